```python
import math
import jax
import jax.numpy as jnp
from jax import lax
import numpy as np

D_MODEL = 1024
BATCH = 32
SEQ = 2048
DEPTH = 4

N_EVEN = (DEPTH + 1) // 2
N_ODD = DEPTH // 2
EPS = 1e-6
POOL_WINDOWS = (2, 4, 8, 16)
N_POOL_GROUPS = 4
POOL_GROUP = D_MODEL // 8
POOL_WIDTH = N_POOL_GROUPS * POOL_GROUP
CONV_WIDTH = D_MODEL // 2
CONV_K = 31
AB_IN = POOL_WIDTH + 2 * CONV_WIDTH
AB_MIX = POOL_WIDTH + CONV_WIDTH
DN_HEADS = 8
DN_HEAD_DIM = 128
DN_WIDTH = DN_HEADS * DN_HEAD_DIM
DN_CONV_K = 4
DN_CHUNK = 64
DN_IN = 4 * DN_WIDTH + 2 * DN_HEADS
N_GROUPS = 4
EXPERTS_PER_GROUP = 8
N_EXPERTS = N_GROUPS * EXPERTS_PER_GROUP
TOP_K = 2
D_EXPERT = D_MODEL // 4
MOE_BLOCK = 128

kernel_name = 'hybrid_pool_conv_deltanet_hmoe'


def rmsnorm(x, gain):
    xf = x.astype(jnp.float32)
    y = xf * lax.rsqrt(jnp.mean(xf * xf, axis=-1, keepdims=True) + EPS)
    return (y * gain.astype(jnp.float32)).astype(x.dtype)


def layernorm(x, gain, bias):
    xf = x.astype(jnp.float32)
    mu = jnp.mean(xf, axis=-1, keepdims=True)
    var = jnp.mean(jnp.square(xf - mu), axis=-1, keepdims=True)
    y = (xf - mu) * lax.rsqrt(var + EPS)
    return (y * gain.astype(jnp.float32) + bias.astype(jnp.float32)).astype(x.dtype)


def l2norm(x):
    return x * lax.rsqrt(jnp.sum(x * x, axis=-1, keepdims=True) + EPS)


def causal_depthwise_conv(x, w):
    k_width, ch = w.shape
    return lax.conv_general_dilated(
        x, w[:, None, :].astype(x.dtype), window_strides=(1,), padding=[(k_width - 1, 0)],
        dimension_numbers=('NWC', 'WIO', 'NWC'), feature_group_count=ch)


def multiscale_causal_pool(u):
    seq = u.shape[1]
    uf = u.astype(jnp.float32)
    cs = jnp.cumsum(uf, axis=1)
    pos = jnp.arange(1, seq + 1, dtype=jnp.float32)
    outs = []
    for j, w in enumerate(POOL_WINDOWS):
        csj = cs[:, :, j]
        win = csj - jnp.pad(csj, ((0, 0), (w, 0), (0, 0)))[:, :seq]
        cnt = jnp.minimum(pos, float(w))[None, :, None]
        outs.append(win / cnt - uf[:, :, j])
    return jnp.stack(outs, axis=2)


def pool_conv_mixer(h, w_in, pool_w, pool_scale, conv_w, conv_b, ln_g, ln_b, w_out):
    bsz, seq, _ = h.shape
    u = h @ w_in
    ua = u[..., :POOL_WIDTH].reshape(bsz, seq, N_POOL_GROUPS, POOL_GROUP)
    pooled = multiscale_causal_pool(ua).astype(h.dtype)
    ya = jnp.einsum('bsgc,gcd->bsgd', pooled, pool_w).reshape(bsz, seq, POOL_WIDTH) * pool_scale
    val = u[..., POOL_WIDTH:POOL_WIDTH + CONV_WIDTH]
    gate = u[..., POOL_WIDTH + CONV_WIDTH:]
    yb = causal_depthwise_conv(val * jax.nn.sigmoid(gate), conv_w) + conv_b
    yb = jax.nn.silu(layernorm(yb, ln_g, ln_b))
    return jnp.concatenate([ya, yb], axis=-1) @ w_out


def chunk_gated_delta_rule(q, k, v, beta, g):
    bsz, seq, nh, dk = q.shape
    dv = v.shape[-1]
    c = DN_CHUNK
    n = seq // c

    def to_chunks(t):
        return t.reshape(bsz, n, c, nh, -1).transpose(0, 3, 1, 2, 4)

    q = to_chunks(q) * (dk ** -0.5)
    k = to_chunks(k)
    v = to_chunks(v)
    beta = to_chunks(beta[..., None])[..., 0]
    g = jnp.cumsum(to_chunks(g[..., None])[..., 0], axis=-1)
    causal = jnp.tril(jnp.ones((c, c), dtype=bool))
    strict = jnp.tril(jnp.ones((c, c), dtype=bool), -1)
    decay = jnp.where(causal, jnp.exp(jnp.where(causal, g[..., :, None] - g[..., None, :], 0.0)), 0.0)
    kb = k * beta[..., None]
    lower = jnp.where(strict, jnp.einsum('bhnck,bhnmk->bhncm', kb, k) * decay, 0.0)
    eye = jnp.eye(c, dtype=jnp.float32)
    t_mat = lax.linalg.triangular_solve(lower + eye, jnp.broadcast_to(eye, lower.shape),
                                        left_side=True, lower=True)
    u = t_mat @ (v * beta[..., None])
    w = t_mat @ (kb * jnp.exp(g)[..., None])
    a_intra = jnp.where(causal, jnp.einsum('bhnck,bhnmk->bhncm', q, k) * decay, 0.0)
    g_last = g[..., -1:]
    qg = q * jnp.exp(g)[..., None]
    kd = k * jnp.exp(g_last - g)[..., None]
    d_last = jnp.exp(g_last[..., 0])

    def step(state, xs):
        qg_i, kd_i, u_i, w_i, a_i, d_i = xs
        v_new = u_i - jnp.einsum('bhck,bhkv->bhcv', w_i, state)
        o_i = jnp.einsum('bhck,bhkv->bhcv', qg_i, state) + jnp.einsum('bhcm,bhmv->bhcv', a_i, v_new)
        state = state * d_i[..., None, None] + jnp.einsum('bhck,bhcv->bhkv', kd_i, v_new)
        return state, o_i

    xs = tuple(jnp.moveaxis(t, 2, 0) for t in (qg, kd, u, w, a_intra, d_last))
    state0 = jnp.zeros((bsz, nh, dk, dv), jnp.float32)
    _, o = lax.scan(step, state0, xs)
    return o.transpose(1, 0, 3, 2, 4).reshape(bsz, seq, nh, dv)


def gated_deltanet_mixer(h, w_in, conv_w, a_log, dt_bias, onorm_g, w_out):
    bsz, seq, _ = h.shape
    u = h @ w_in
    qkv = jax.nn.silu(causal_depthwise_conv(u[..., :3 * DN_WIDTH], conv_w)).astype(jnp.float32)
    q, k, v = [t.reshape(bsz, seq, DN_HEADS, DN_HEAD_DIM) for t in jnp.split(qkv, 3, axis=-1)]
    z = u[..., 3 * DN_WIDTH:4 * DN_WIDTH].reshape(bsz, seq, DN_HEADS, DN_HEAD_DIM).astype(jnp.float32)
    b_raw = u[..., 4 * DN_WIDTH:4 * DN_WIDTH + DN_HEADS].astype(jnp.float32)
    a_raw = u[..., 4 * DN_WIDTH + DN_HEADS:].astype(jnp.float32)
    beta = jax.nn.sigmoid(b_raw)
    g = -jnp.exp(a_log.astype(jnp.float32)) * jax.nn.softplus(a_raw + dt_bias.astype(jnp.float32))
    o = chunk_gated_delta_rule(l2norm(q), l2norm(k), v, beta, g)
    o = o * lax.rsqrt(jnp.mean(o * o, axis=-1, keepdims=True) + EPS) * onorm_g.astype(jnp.float32) * jax.nn.silu(z)
    return o.reshape(bsz, seq, DN_WIDTH).astype(h.dtype) @ w_out


def hierarchical_moe(h, w_grp, b_grp, w_exp, b_exp, w_up, w_down):
    bsz, seq, d = h.shape
    n_tok = bsz * seq
    xt = h.reshape(n_tok, d)
    grp_prob = jax.nn.softmax((xt @ w_grp).astype(jnp.float32) + b_grp.astype(jnp.float32), axis=-1)
    gp, gi = lax.top_k(grp_prob, 1)
    e_logits = ((xt @ w_exp).astype(jnp.float32) + b_exp.astype(jnp.float32)).reshape(n_tok, N_GROUPS, EXPERTS_PER_GROUP)
    sel = e_logits[jnp.arange(n_tok), gi[:, 0]]
    ew, ei = lax.top_k(jax.nn.softmax(sel, axis=-1), TOP_K)
    gate = gp * (ew / jnp.sum(ew, axis=-1, keepdims=True))
    eid = (gi * EXPERTS_PER_GROUP + ei).reshape(-1)
    tok = jnp.repeat(jnp.arange(n_tok, dtype=jnp.int32), TOP_K)
    wts = gate.reshape(-1)
    m = n_tok * TOP_K
    order = jnp.argsort(eid)
    se = eid[order]
    counts = jnp.bincount(eid, length=N_EXPERTS)
    starts = jnp.cumsum(counts) - counts
    padded = (counts + MOE_BLOCK - 1) // MOE_BLOCK * MOE_BLOCK
    pend = jnp.cumsum(padded)
    pstart = pend - padded
    dest = pstart[se] + jnp.arange(m) - starts[se]
    p_rows = (m + N_EXPERTS * (MOE_BLOCK - 1) + MOE_BLOCK - 1) // MOE_BLOCK * MOE_BLOCK
    n_blocks = p_rows // MOE_BLOCK
    buf_tok = jnp.zeros((p_rows,), jnp.int32).at[dest].set(tok[order])
    buf_w = jnp.zeros((p_rows,), jnp.float32).at[dest].set(wts[order])
    blk_e = jnp.minimum(jnp.searchsorted(pend, jnp.arange(n_blocks) * MOE_BLOCK, side='right'), N_EXPERTS - 1)

    def expert_block(args):
        bt, bw, be = args
        xb = xt[bt]
        gu = xb @ w_up[be]
        y = (jax.nn.silu(gu[:, :D_EXPERT]) * gu[:, D_EXPERT:]) @ w_down[be]
        return y * bw[:, None].astype(y.dtype)

    yb = lax.map(expert_block, (buf_tok.reshape(n_blocks, MOE_BLOCK), buf_w.reshape(n_blocks, MOE_BLOCK), blk_e))
    out = jnp.zeros_like(xt).at[buf_tok].add(yb.reshape(p_rows, d))
    return out.reshape(bsz, seq, d)


def setup_inputs(seed: int = 0) -> dict:
    key = jax.random.key(seed)
    ks = iter(jax.random.split(key, 40))

    def nrm(shape, scale):
        return jax.random.normal(next(ks), shape, jnp.float32) * scale

    dt = jnp.exp(jax.random.uniform(next(ks), (N_ODD, DN_HEADS), jnp.float32, math.log(1e-3), math.log(1e-1)))
    return {
        'x': nrm((BATCH, SEQ, D_MODEL), 1.0),
        'c': nrm((BATCH, D_MODEL), 1.0),
        'mod_w': nrm((DEPTH, D_MODEL, 6 * D_MODEL), 0.5 * D_MODEL ** -0.5),
        'mod_b': nrm((DEPTH, 6 * D_MODEL), 0.01),
        'norm_mix': 1.0 + nrm((DEPTH, D_MODEL), 0.02),
        'norm_ffn': 1.0 + nrm((DEPTH, D_MODEL), 0.02),
        'ab_w_in': nrm((N_EVEN, D_MODEL, AB_IN), D_MODEL ** -0.5),
        'pool_w': nrm((N_EVEN, N_POOL_GROUPS, POOL_GROUP, POOL_GROUP), POOL_GROUP ** -0.5),
        'pool_scale': 1.0 + nrm((N_EVEN, POOL_WIDTH), 0.02),
        'conv_w': nrm((N_EVEN, CONV_K, CONV_WIDTH), CONV_K ** -0.5),
        'conv_b': nrm((N_EVEN, CONV_WIDTH), 0.01),
        'conv_ln_g': 1.0 + nrm((N_EVEN, CONV_WIDTH), 0.02),
        'conv_ln_b': nrm((N_EVEN, CONV_WIDTH), 0.01),
        'ab_w_out': nrm((N_EVEN, AB_MIX, D_MODEL), AB_MIX ** -0.5),
        'dn_w_in': nrm((N_ODD, D_MODEL, DN_IN), D_MODEL ** -0.5),
        'dn_conv_w': nrm((N_ODD, DN_CONV_K, 3 * DN_WIDTH), DN_CONV_K ** -0.5),
        'dn_a_log': jnp.log(jax.random.uniform(next(ks), (N_ODD, DN_HEADS), jnp.float32, 1.0, 16.0)),
        'dn_dt_bias': dt + jnp.log(-jnp.expm1(-dt)),
        'dn_onorm': 1.0 + nrm((N_ODD, DN_HEAD_DIM), 0.02),
        'dn_w_out': nrm((N_ODD, DN_WIDTH, D_MODEL), DN_WIDTH ** -0.5),
        'moe_w_grp': nrm((DEPTH, D_MODEL, N_GROUPS), D_MODEL ** -0.5),
        'moe_b_grp': nrm((DEPTH, N_GROUPS), 0.01),
        'moe_w_exp': nrm((DEPTH, D_MODEL, N_EXPERTS), D_MODEL ** -0.5),
        'moe_b_exp': nrm((DEPTH, N_EXPERTS), 0.01),
        'moe_w_up': nrm((DEPTH, N_EXPERTS, D_MODEL, 2 * D_EXPERT), D_MODEL ** -0.5),
        'moe_w_down': nrm((DEPTH, N_EXPERTS, D_EXPERT, D_MODEL), D_EXPERT ** -0.5),
        'final_norm': 1.0 + nrm((D_MODEL,), 0.02),
    }


def reference(x, c, mod_w, mod_b, norm_mix, norm_ffn, ab_w_in, pool_w, pool_scale, conv_w, conv_b,
              conv_ln_g, conv_ln_b, ab_w_out, dn_w_in, dn_conv_w, dn_a_log, dn_dt_bias, dn_onorm, dn_w_out,
              moe_w_grp, moe_b_grp, moe_w_exp, moe_b_exp, moe_w_up, moe_w_down, final_norm):
    c_act = jax.nn.silu(c)
    for l in range(DEPTH):
        mod = (c_act @ mod_w[l] + mod_b[l])[:, None, :]
        sh_m, sc_m, g_m, sh_f, sc_f, g_f = jnp.split(mod, 6, axis=-1)
        hn = rmsnorm(x, norm_mix[l]) * (1.0 + sc_m) + sh_m
        i = l // 2
        if l % 2 == 0:
            y = pool_conv_mixer(hn, ab_w_in[i], pool_w[i], pool_scale[i], conv_w[i], conv_b[i],
                                conv_ln_g[i], conv_ln_b[i], ab_w_out[i])
        else:
            y = gated_deltanet_mixer(hn, dn_w_in[i], dn_conv_w[i], dn_a_log[i], dn_dt_bias[i],
                                     dn_onorm[i], dn_w_out[i])
        x = x + g_m * y
        hn = rmsnorm(x, norm_ffn[l]) * (1.0 + sc_f) + sh_f
        x = x + g_f * hierarchical_moe(hn, moe_w_grp[l], moe_b_grp[l], moe_w_exp[l], moe_b_exp[l],
                                       moe_w_up[l], moe_w_down[l])
    return rmsnorm(x, final_norm)
```

```python
import functools

import jax
import jax.numpy as jnp
from jax import lax
from jax.experimental import pallas as pl
from jax.experimental.pallas import tpu as pltpu

F32 = jnp.float32
BF16 = jnp.bfloat16
HI = lax.Precision.HIGHEST

D_MODEL = 1024
EPS = 1e-6
LANES = 128
SUBLANES = 8
ROW_TILES = D_MODEL // LANES
VMEM_LIMIT = 56 * 1024 * 1024

POOL_WINDOWS = (2, 4, 8, 16)
POOL_GROUP = 128
POOL_WIDTH = 512
CONV_WIDTH = 512
CONV_K = 31
AB_IN = POOL_WIDTH + 2 * CONV_WIDTH
HALO = 32

DN_HEADS = 8
DN_DIM = 128
DN_WIDTH = DN_HEADS * DN_DIM
DN_CONV_K = 4
DN_CHUNK = 128
DN_HALO = 8

N_GROUPS = 4
EPG = 8
N_EXPERTS = N_GROUPS * EPG
TOP_K = 2
D_EXPERT = 256
MOE_BLOCK = 128
ROUTER_E0 = N_GROUPS

TS_EVEN = 512
TS_ODD = 256
GATHER_ROWS = 128

NEG = -1e30


def _dot(a, b):
    return jnp.dot(a, b, preferred_element_type=F32)


def _dot_nt(a, b):
    return lax.dot_general(a, b, (((1,), (1,)), ((), ())), preferred_element_type=F32)


def _dot_tn(a, b):
    return lax.dot_general(a, b, (((0,), (0,)), ((), ())), preferred_element_type=F32)


def _rms_mod(x, gain, scale, shift):
    ms = jnp.mean(x * x, axis=-1, keepdims=True)
    return x * lax.rsqrt(ms + EPS) * gain * (1.0 + scale) + shift


def _load_rows(ref, n, first, stride):
    return jnp.concatenate([ref[pl.ds(first + j, n, stride=stride), :] for j in range(ROW_TILES)], axis=1)


def _store_rows(ref, val, n):
    for j in range(ROW_TILES):
        ref[pl.ds(j, n, stride=ROW_TILES), :] = val[:, j * LANES:(j + 1) * LANES]


def _mod_kernel(c_ref, w_ref, b_ref, o_ref):
    ca = jax.nn.silu(c_ref[...])
    o_ref[...] = jnp.dot(ca, w_ref[...], precision=HI, preferred_element_type=F32) + b_ref[...]


def _modulation(c, mod_w, mod_b):
    depth = mod_w.shape[0]
    bsz = c.shape[0]
    n_col = mod_w.shape[2] // D_MODEL
    out = pl.pallas_call(
        _mod_kernel,
        grid=(depth, n_col),
        in_specs=[
            pl.BlockSpec((bsz, D_MODEL), lambda l, j: (0, 0)),
            pl.BlockSpec((None, D_MODEL, D_MODEL), lambda l, j: (l, 0, j)),
            pl.BlockSpec((None, 1, D_MODEL), lambda l, j: (l, 0, j)),
        ],
        out_specs=pl.BlockSpec((None, bsz, D_MODEL), lambda l, j: (l, 0, j)),
        out_shape=jax.ShapeDtypeStruct((depth, bsz, n_col * D_MODEL), F32),
        compiler_params=pltpu.CompilerParams(dimension_semantics=("arbitrary", "arbitrary"),
                                             vmem_limit_bytes=VMEM_LIMIT),
        name="modulation",
    )(c, mod_w, mod_b.reshape(depth, 1, n_col * D_MODEL))
    mod = out.reshape(depth, bsz, n_col, D_MODEL)
    return jnp.pad(mod, ((0, 0), (0, 0), (0, SUBLANES - n_col), (0, 0)))


def _residual_in(x_ref, prev, ts):
    x = x_ref[...]
    if prev is None:
        return x
    yt_ref, pg_ref, pmod_ref = prev
    y0 = _load_rows(yt_ref, ts, 0, TOP_K * ROW_TILES)
    y1 = _load_rows(yt_ref, ts, ROW_TILES, TOP_K * ROW_TILES)
    pg = pg_ref[...]
    return x + pmod_ref[5:6, :] * (y0 * pg[:, 0:1] + y1 * pg[:, 1:2])


def _ffn_prep(x_new, first_step, mod_ref, nffn_ref, wr_ref, br_ref, cnt_ref, hn2_ref, ids_ref, gates_ref,
              counts_ref, ts):
    hn2 = _rms_mod(x_new, nffn_ref[...], mod_ref[4:5, :], mod_ref[3:4, :])
    _store_rows(hn2_ref, hn2, ts)
    lg = jnp.dot(hn2, wr_ref[...], precision=HI, preferred_element_type=F32) + br_ref[...]
    lane = lax.broadcasted_iota(jnp.int32, (ts, LANES), 1)
    is_grp = lane < N_GROUPS
    gl = jnp.where(is_grp, lg, NEG)
    gmax = jnp.max(gl, axis=-1, keepdims=True)
    gsum = jnp.sum(jnp.where(is_grp, jnp.exp(gl - gmax), 0.0), axis=-1, keepdims=True)
    gp = 1.0 / gsum
    gi = jnp.min(jnp.where(gl == gmax, lane, LANES), axis=-1, keepdims=True)
    lo = ROUTER_E0 + gi * EPG
    in_grp = (lane >= lo) & (lane < lo + EPG)
    el = jnp.where(in_grp, lg, NEG)
    m1 = jnp.max(el, axis=-1, keepdims=True)
    i1 = jnp.min(jnp.where(el == m1, lane, LANES), axis=-1, keepdims=True)
    el2 = jnp.where(lane == i1, NEG, el)
    m2 = jnp.max(el2, axis=-1, keepdims=True)
    i2 = jnp.min(jnp.where(el2 == m2, lane, LANES), axis=-1, keepdims=True)
    e2 = jnp.exp(m2 - m1)
    w1 = gp / (1.0 + e2)
    w2 = gp * e2 / (1.0 + e2)

    @pl.when(first_step)
    def _():
        cnt_ref[...] = jnp.zeros_like(cnt_ref)

    onehot = ((lane == i1) | (lane == i2)).astype(F32)
    r = lax.broadcasted_iota(jnp.int32, (ts, ts), 0)
    c = lax.broadcasted_iota(jnp.int32, (ts, ts), 1)
    before = (r > c).astype(BF16)
    run = _dot(before, onehot.astype(BF16)) + cnt_ref[0:1, :]
    rank1 = jnp.sum(jnp.where(lane == i1, run, 0.0), axis=-1, keepdims=True)
    rank2 = jnp.sum(jnp.where(lane == i2, run, 0.0), axis=-1, keepdims=True)
    total = cnt_ref[0:1, :] + jnp.sum(onehot, axis=0, keepdims=True)
    cnt_ref[...] = jnp.broadcast_to(total, cnt_ref.shape)
    counts_ref[...] = jnp.broadcast_to(total, counts_ref.shape)

    ids = jnp.where(lane == 0, i1 - ROUTER_E0,
                    jnp.where(lane == 1, i2 - ROUTER_E0,
                              jnp.where(lane == 2, rank1.astype(jnp.int32),
                                        jnp.where(lane == 3, rank2.astype(jnp.int32), 0))))
    ids_ref[...] = ids
    gates_ref[...] = jnp.where(lane == 0, w1, jnp.where(lane == 1, w2, 0.0))


def _mixer_call(body, ts, x, prev, mod_l, params, scratch, name):
    n_tok = x.shape[0]
    bsz = mod_l.shape[0]
    seq = n_tok // bsz
    n_s = seq // ts
    tok_map = lambda b, s: (b * n_s + s, 0)
    in_specs = [pl.BlockSpec((ts, D_MODEL), tok_map)]
    args = [x]
    if prev is not None:
        yt, pgates, pmod = prev
        in_specs += [pl.BlockSpec((ts * TOP_K * ROW_TILES, LANES), tok_map),
                     pl.BlockSpec((ts, LANES), tok_map),
                     pl.BlockSpec((None, SUBLANES, D_MODEL), lambda b, s: (b, 0, 0))]
        args += [yt, pgates, pmod]
    in_specs.append(pl.BlockSpec((None, SUBLANES, D_MODEL), lambda b, s: (b, 0, 0)))
    args.append(mod_l)
    for arr in params:
        nd = arr.ndim
        in_specs.append(pl.BlockSpec(arr.shape, lambda b, s, _nd=nd: (0,) * _nd))
        args.append(arr)
    out_shape = [
        jax.ShapeDtypeStruct((n_tok, D_MODEL), F32),
        jax.ShapeDtypeStruct((n_tok * ROW_TILES, LANES), F32),
        jax.ShapeDtypeStruct((n_tok, LANES), jnp.int32),
        jax.ShapeDtypeStruct((n_tok, LANES), F32),
        jax.ShapeDtypeStruct((SUBLANES, LANES), F32),
    ]
    out_specs = [
        pl.BlockSpec((ts, D_MODEL), tok_map),
        pl.BlockSpec((ts * ROW_TILES, LANES), tok_map),
        pl.BlockSpec((ts, LANES), tok_map),
        pl.BlockSpec((ts, LANES), tok_map),
        pl.BlockSpec((SUBLANES, LANES), lambda b, s: (0, 0)),
    ]
    return pl.pallas_call(
        functools.partial(body, prev is not None, ts),
        grid=(bsz, n_s),
        in_specs=in_specs,
        out_specs=out_specs,
        out_shape=out_shape,
        scratch_shapes=scratch,
        compiler_params=pltpu.CompilerParams(dimension_semantics=("arbitrary", "arbitrary"),
                                             vmem_limit_bytes=VMEM_LIMIT),
        name=name,
    )(*args)


def _even_kernel(has_prev, ts, *refs):
    n_prev = 3 if has_prev else 0
    x_ref = refs[0]
    prev = refs[1:1 + n_prev] if has_prev else None
    (mod_ref, nmix_ref, nffn_ref, win_ref, poolw_ref, pscale_ref, convw_ref, convb_ref, lng_ref, lnb_ref,
     wout_ref, wr_ref, br_ref) = refs[1 + n_prev:14 + n_prev]
    xo_ref, hn2_ref, ids_ref, gates_ref, counts_ref = refs[14 + n_prev:19 + n_prev]
    extp_ref, extc_ref, shift_ref, mix_ref, cnt_ref = refs[19 + n_prev:]

    b = pl.program_id(0)
    s = pl.program_id(1)
    x_in = _residual_in(x_ref, prev, ts)
    hn = _rms_mod(x_in, nmix_ref[...], mod_ref[1:2, :], mod_ref[0:1, :])
    u = _dot(hn.astype(BF16), win_ref[...])

    @pl.when(s == 0)
    def _():
        extp_ref[0:HALO, :] = jnp.zeros((HALO, POOL_WIDTH), F32)
        extc_ref[0:HALO, :] = jnp.zeros((HALO, CONV_WIDTH), F32)

    @pl.when(s > 0)
    def _():
        extp_ref[0:HALO, :] = extp_ref[ts:ts + HALO, :]
        extc_ref[0:HALO, :] = extc_ref[ts:ts + HALO, :]

    extp_ref[HALO:HALO + ts, :] = u[:, :POOL_WIDTH]
    extc_ref[HALO:HALO + ts, :] = (u[:, POOL_WIDTH:POOL_WIDTH + CONV_WIDTH]
                                   * jax.nn.sigmoid(u[:, POOL_WIDTH + CONV_WIDTH:]))

    pos = (s * ts + 1 + lax.broadcasted_iota(jnp.int32, (ts, 1), 0)).astype(F32)
    for j, w in enumerate(POOL_WINDOWS):
        c0 = j * POOL_GROUP
        a = extp_ref[HALO:HALO + ts, c0:c0 + POOL_GROUP]
        win = a
        for d in range(1, w):
            win = win + extp_ref[HALO - d:HALO - d + ts, c0:c0 + POOL_GROUP]
        pooled = win / jnp.minimum(pos, float(w)) - a
        ya = _dot(pooled.astype(BF16), poolw_ref[j]) * pscale_ref[:, c0:c0 + POOL_GROUP]
        mix_ref[:, c0:c0 + POOL_GROUP] = ya.astype(BF16)

    for sh in range(1, SUBLANES):
        shift_ref[sh - 1, SUBLANES:HALO + ts, :] = extc_ref[SUBLANES - sh:HALO + ts - sh, :]
    rc = 32

    def conv_chunk(ci, carry):
        base = pl.multiple_of(ci * rc, rc)
        acc = jnp.broadcast_to(convb_ref[...], (rc, CONV_WIDTH))
        for k in range(CONV_K):
            delay = CONV_K - 1 - k
            a8, sh = delay // SUBLANES, delay % SUBLANES
            rows = pl.ds(base + (HALO - a8 * SUBLANES), rc)
            tap = extc_ref[rows, :] if sh == 0 else shift_ref[sh - 1, rows, :]
            acc = acc + convw_ref[k:k + 1, :] * tap
        mu = jnp.mean(acc, axis=-1, keepdims=True)
        cen = acc - mu
        var = jnp.mean(cen * cen, axis=-1, keepdims=True)
        yb = cen * lax.rsqrt(var + EPS) * lng_ref[...] + lnb_ref[...]
        mix_ref[pl.ds(base, rc), POOL_WIDTH:POOL_WIDTH + CONV_WIDTH] = jax.nn.silu(yb).astype(BF16)
        return carry

    lax.fori_loop(0, ts // rc, conv_chunk, 0)

    y = _dot(mix_ref[...], wout_ref[...])
    x_new = x_in + mod_ref[2:3, :] * y
    xo_ref[...] = x_new
    _ffn_prep(x_new, (b == 0) & (s == 0), mod_ref, nffn_ref, wr_ref, br_ref, cnt_ref, hn2_ref, ids_ref,
              gates_ref, counts_ref, ts)


def _even_layer(x, prev, mod_l, nmix, nffn, w_in, pool_w, pool_scale, conv_w, conv_b, ln_g, ln_b, w_out, w_r,
                b_r):
    ts = TS_EVEN
    params = [nmix.reshape(1, -1), nffn.reshape(1, -1), w_in.astype(BF16), pool_w.astype(BF16),
              pool_scale.reshape(1, -1), jnp.pad(conv_w, ((0, HALO - CONV_K), (0, 0))), conv_b.reshape(1, -1),
              ln_g.reshape(1, -1), ln_b.reshape(1, -1), w_out.astype(BF16), w_r, b_r]
    scratch = [pltpu.VMEM((HALO + ts, POOL_WIDTH), F32), pltpu.VMEM((HALO + ts, CONV_WIDTH), F32),
               pltpu.VMEM((SUBLANES - 1, HALO + ts, CONV_WIDTH), F32),
               pltpu.VMEM((ts, D_MODEL), BF16), pltpu.VMEM((SUBLANES, LANES), F32)]
    return _mixer_call(_even_kernel, ts, x, prev, mod_l, params, scratch, "even_mixer")


def _odd_kernel(has_prev, ts, *refs):
    n_prev = 3 if has_prev else 0
    x_ref = refs[0]
    prev = refs[1:1 + n_prev] if has_prev else None
    (mod_ref, nmix_ref, nffn_ref, win_ref, wba_ref, convw_ref, alog_ref, dtb_ref, onorm_ref, wout_ref, wr_ref,
     br_ref) = refs[1 + n_prev:13 + n_prev]
    xo_ref, hn2_ref, ids_ref, gates_ref, counts_ref = refs[13 + n_prev:18 + n_prev]
    (ext_ref, q_ref, k_ref, v_ref, z_ref, gcol_ref, bcol_ref, grow_ref, og_ref, state_ref, mix_ref,
     cnt_ref) = refs[18 + n_prev:]

    b = pl.program_id(0)
    s = pl.program_id(1)
    n_ch = ts // DN_CHUNK
    x_in = _residual_in(x_ref, prev, ts)
    hn = _rms_mod(x_in, nmix_ref[...], mod_ref[1:2, :], mod_ref[0:1, :])
    hb = hn.astype(BF16)

    @pl.when(s == 0)
    def _():
        ext_ref[0:DN_HALO, :] = jnp.zeros((DN_HALO, 3 * DN_WIDTH), F32)
        state_ref[...] = jnp.zeros_like(state_ref)

    @pl.when(s > 0)
    def _():
        ext_ref[0:DN_HALO, :] = ext_ref[ts:ts + DN_HALO, :]

    for g in range(3):
        ext_ref[DN_HALO:DN_HALO + ts, g * DN_WIDTH:(g + 1) * DN_WIDTH] = _dot(
            hb, win_ref[:, g * DN_WIDTH:(g + 1) * DN_WIDTH])
    z = _dot(hb, win_ref[:, 3 * DN_WIDTH:4 * DN_WIDTH])
    for h in range(DN_HEADS):
        z_ref[h] = z[:, h * DN_DIM:(h + 1) * DN_DIM]

    for g in range(3):
        for h in range(DN_HEADS):
            c0 = g * DN_WIDTH + h * DN_DIM
            acc = None
            for k in range(DN_CONV_K):
                off = DN_HALO - (DN_CONV_K - 1) + k
                term = convw_ref[k:k + 1, c0:c0 + DN_DIM] * ext_ref[off:off + ts, c0:c0 + DN_DIM]
                acc = term if acc is None else acc + term
            cv = jax.nn.silu(acc)
            if g == 0:
                q_ref[h] = cv * lax.rsqrt(jnp.sum(cv * cv, axis=-1, keepdims=True) + EPS) * (DN_DIM ** -0.5)
            elif g == 1:
                k_ref[h] = cv * lax.rsqrt(jnp.sum(cv * cv, axis=-1, keepdims=True) + EPS)
            else:
                v_ref[h] = cv

    ba = jnp.dot(hn, wba_ref[...], precision=HI, preferred_element_type=F32)
    beta = jax.nn.sigmoid(ba)
    gdec = -jnp.exp(alog_ref[...]) * jax.nn.softplus(ba + dtb_ref[...])
    r = lax.broadcasted_iota(jnp.int32, (ts, ts), 0)
    c = lax.broadcasted_iota(jnp.int32, (ts, ts), 1)
    same_chunk = (r // DN_CHUNK) == (c // DN_CHUNK)
    tri = ((r >= c) & same_chunk).astype(F32)
    gc = jnp.dot(tri, gdec, precision=HI, preferred_element_type=F32)
    gct = gc.T
    for h in range(DN_HEADS):
        gcol_ref[h] = jnp.broadcast_to(gc[:, DN_HEADS + h:DN_HEADS + h + 1], (ts, LANES))
        bcol_ref[h] = jnp.broadcast_to(beta[:, h:h + 1], (ts, LANES))
        grow_ref[h] = jnp.broadcast_to(gct[DN_HEADS + h:DN_HEADS + h + 1, :], (SUBLANES, ts))

    ri = lax.broadcasted_iota(jnp.int32, (DN_CHUNK, DN_CHUNK), 0)
    ci = lax.broadcasted_iota(jnp.int32, (DN_CHUNK, DN_CHUNK), 1)
    causal = ri >= ci
    strict = ri > ci
    eye = (ri == ci).astype(F32)

    ci_pair = lax.broadcasted_iota(jnp.int32, (DN_CHUNK, 2 * DN_CHUNK), 1) & (DN_CHUNK - 1)
    ri_pair = lax.broadcasted_iota(jnp.int32, (DN_CHUNK, 2 * DN_CHUNK), 0)
    eye_pair = (ri_pair == ci_pair).astype(F32)
    zero_blk = jnp.zeros((DN_CHUNK, DN_CHUNK), BF16)

    def block_diag(p):
        return jnp.concatenate([jnp.concatenate([p[:, :DN_CHUNK], zero_blk], axis=1),
                                jnp.concatenate([zero_blk, p[:, DN_CHUNK:]], axis=1)], axis=0)

    def pair_body(h, ch0, st):
        per_chunk = []
        for ch in (ch0, ch0 + 1):
            r0 = ch * DN_CHUNK
            q = q_ref[h, r0:r0 + DN_CHUNK, :]
            k = k_ref[h, r0:r0 + DN_CHUNK, :]
            v = v_ref[h, r0:r0 + DN_CHUNK, :]
            gcl = gcol_ref[h, r0:r0 + DN_CHUNK, :]
            grw = grow_ref[h, 0:1, r0:r0 + DN_CHUNK]
            bt = bcol_ref[h, r0:r0 + DN_CHUNK, :]
            decay = jnp.where(causal, jnp.exp(jnp.where(causal, gcl - grw, 0.0)), 0.0)
            kb = k * bt
            kbf = k.astype(BF16)
            nl = jnp.where(strict, -(_dot_nt(kb.astype(BF16), kbf) * decay), 0.0)
            attn = jnp.where(causal, _dot_nt(q.astype(BF16), kbf) * decay, 0.0)
            per_chunk.append((q, k, v, gcl, bt, kb, nl, attn, r0))
        nl2 = jnp.concatenate([per_chunk[0][6], per_chunk[1][6]], axis=1)
        tm = eye_pair + jnp.where((ri_pair >> 1) == (ci_pair >> 1), nl2, 0.0)
        blk = 2
        while blk < DN_CHUNK:
            sel = (((ri_pair // (2 * blk)) == (ci_pair // (2 * blk)))
                   & ((ri_pair & (2 * blk - 1)) >= blk) & ((ci_pair & (2 * blk - 1)) < blk))
            nb = jnp.where(sel, nl2, 0.0).astype(BF16)
            tmb = tm.astype(BF16)
            y = _dot(nb, block_diag(tmb))
            tm = tm + _dot(tmb, block_diag(y.astype(BF16)))
            blk *= 2
        for idx, (q, k, v, gcl, bt, kb, nl, attn, r0) in enumerate(per_chunk):
            t_c = tm[:, idx * DN_CHUNK:(idx + 1) * DN_CHUNK]
            eg = jnp.exp(gcl)
            rhs = jnp.concatenate([v * bt, kb * eg], axis=1).astype(BF16)
            uw = _dot(t_c.astype(BF16), rhs)
            u_c = uw[:, :DN_DIM]
            w_c = uw[:, DN_DIM:]
            glast = gcl[DN_CHUNK - 1:DN_CHUNK, :]
            sb = st.astype(BF16)
            v_new = u_c - _dot(w_c.astype(BF16), sb)
            vnb = v_new.astype(BF16)
            o = _dot((q * eg).astype(BF16), sb) + _dot(attn.astype(BF16), vnb)
            st = st * jnp.exp(glast) + _dot_tn((k * jnp.exp(glast - gcl)).astype(BF16), vnb)
            zz = z_ref[h, r0:r0 + DN_CHUNK, :]
            o = o * lax.rsqrt(jnp.mean(o * o, axis=-1, keepdims=True) + EPS) * onorm_ref[...] * jax.nn.silu(zz)
            og_ref[h, r0:r0 + DN_CHUNK, :] = o
        return st

    def head_body(h, carry):
        st = state_ref[h]
        for ch0 in range(0, n_ch, 2):
            st = pair_body(h, ch0, st)
        state_ref[h] = st
        return carry

    lax.fori_loop(0, DN_HEADS, head_body, 0)

    for h in range(DN_HEADS):
        mix_ref[:, h * DN_DIM:(h + 1) * DN_DIM] = og_ref[h].astype(BF16)
    y = _dot(mix_ref[...], wout_ref[...])
    x_new = x_in + mod_ref[2:3, :] * y
    xo_ref[...] = x_new
    _ffn_prep(x_new, (b == 0) & (s == 0), mod_ref, nffn_ref, wr_ref, br_ref, cnt_ref, hn2_ref, ids_ref,
              gates_ref, counts_ref, ts)


def _odd_layer(x, prev, mod_l, nmix, nffn, w_in, conv_w, a_log, dt_bias, onorm, w_out, w_r, b_r):
    ts = TS_ODD
    w_main = w_in[:, :4 * DN_WIDTH].astype(BF16)
    w_ba = jnp.pad(w_in[:, 4 * DN_WIDTH:], ((0, 0), (0, LANES - 2 * DN_HEADS)))
    lane_pad = (DN_HEADS, LANES - 2 * DN_HEADS)
    params = [nmix.reshape(1, -1), nffn.reshape(1, -1), w_main, w_ba,
              jnp.pad(conv_w, ((0, SUBLANES - DN_CONV_K), (0, 0))),
              jnp.pad(a_log, lane_pad).reshape(1, -1), jnp.pad(dt_bias, lane_pad).reshape(1, -1),
              onorm.reshape(1, -1), w_out.astype(BF16), w_r, b_r]
    head = (DN_HEADS, ts, DN_DIM)
    scratch = [pltpu.VMEM((DN_HALO + ts, 3 * DN_WIDTH), F32),
               pltpu.VMEM(head, F32), pltpu.VMEM(head, F32), pltpu.VMEM(head, F32), pltpu.VMEM(head, F32),
               pltpu.VMEM(head, F32), pltpu.VMEM(head, F32), pltpu.VMEM((DN_HEADS, SUBLANES, ts), F32),
               pltpu.VMEM(head, F32), pltpu.VMEM((DN_HEADS, DN_DIM, DN_DIM), F32),
               pltpu.VMEM((ts, D_MODEL), BF16), pltpu.VMEM((SUBLANES, LANES), F32)]
    return _mixer_call(_odd_kernel, ts, x, prev, mod_l, params, scratch, "deltanet_mixer")


def _gather_kernel(idx_ref, src_ref, dst_ref, sem):
    i = pl.program_id(0)

    def row_copy(r):
        t = idx_ref[0, r]
        src = src_ref.at[pl.ds(pl.multiple_of(t * ROW_TILES, ROW_TILES), ROW_TILES), :]
        dst = dst_ref.at[pl.ds(pl.multiple_of((i * GATHER_ROWS + r) * ROW_TILES, ROW_TILES), ROW_TILES), :]
        return pltpu.make_async_copy(src, dst, sem)

    def issue(r, carry):
        row_copy(r).start()
        return carry

    def drain(r, carry):
        row_copy(r).wait()
        return carry

    lax.fori_loop(0, GATHER_ROWS, issue, 0)
    lax.fori_loop(0, GATHER_ROWS, drain, 0)


def _gather_rows(src, idx):
    n_dst = idx.shape[0]
    n_blk = n_dst // GATHER_ROWS
    return pl.pallas_call(
        _gather_kernel,
        grid=(n_blk,),
        in_specs=[pl.BlockSpec((None, 1, GATHER_ROWS), lambda i: (i, 0, 0), memory_space=pltpu.SMEM),
                  pl.BlockSpec(memory_space=pl.ANY)],
        out_specs=pl.BlockSpec(memory_space=pl.ANY),
        out_shape=jax.ShapeDtypeStruct((n_dst * ROW_TILES, LANES), F32),
        scratch_shapes=[pltpu.SemaphoreType.DMA],
        compiler_params=pltpu.CompilerParams(dimension_semantics=("arbitrary",)),
        name="row_gather",
    )(idx.reshape(n_blk, 1, GATHER_ROWS), src)


def _expert_kernel(blk_e_ref, xs_ref, wup_ref, wdn_ref, y_ref, xb_ref):
    for j in range(ROW_TILES):
        xb_ref[:, j * LANES:(j + 1) * LANES] = xs_ref[pl.ds(j, MOE_BLOCK, stride=ROW_TILES), :].astype(BF16)
    gu = _dot(xb_ref[...], wup_ref[...])
    hid = jax.nn.silu(gu[:, :D_EXPERT]) * gu[:, D_EXPERT:]
    y = _dot(hid.astype(BF16), wdn_ref[...])
    _store_rows(y_ref, y, MOE_BLOCK)


def _expert_mlp(xs, blk_e, w_up, w_down):
    n_blk = blk_e.shape[0]
    grid_spec = pltpu.PrefetchScalarGridSpec(
        num_scalar_prefetch=1,
        grid=(n_blk,),
        in_specs=[pl.BlockSpec((MOE_BLOCK * ROW_TILES, LANES), lambda i, be: (i, 0)),
                  pl.BlockSpec((None, D_MODEL, 2 * D_EXPERT), lambda i, be: (be[i], 0, 0)),
                  pl.BlockSpec((None, D_EXPERT, D_MODEL), lambda i, be: (be[i], 0, 0))],
        out_specs=pl.BlockSpec((MOE_BLOCK * ROW_TILES, LANES), lambda i, be: (i, 0)),
        scratch_shapes=[pltpu.VMEM((MOE_BLOCK, D_MODEL), BF16)],
    )
    return pl.pallas_call(
        _expert_kernel,
        grid_spec=grid_spec,
        out_shape=jax.ShapeDtypeStruct(xs.shape, F32),
        compiler_params=pltpu.CompilerParams(dimension_semantics=("arbitrary",), vmem_limit_bytes=VMEM_LIMIT),
        name="expert_mlp",
    )(blk_e, xs, w_up, w_down)


def _moe(hn2, ids, counts, w_up, w_down):
    n_tok = ids.shape[0]
    m = n_tok * TOP_K
    p_rows = (m + N_EXPERTS * (MOE_BLOCK - 1) + MOE_BLOCK - 1) // MOE_BLOCK * MOE_BLOCK
    n_blk = p_rows // MOE_BLOCK
    eid = ids[:, 0:TOP_K]
    rank = ids[:, TOP_K:2 * TOP_K]
    cnt = counts[0, ROUTER_E0:ROUTER_E0 + N_EXPERTS].astype(jnp.int32)
    padded = (cnt + MOE_BLOCK - 1) // MOE_BLOCK * MOE_BLOCK
    pend = jnp.cumsum(padded)
    pstart = pend - padded
    dest = (pstart[eid] + rank).reshape(-1)
    tok = jnp.repeat(jnp.arange(n_tok, dtype=jnp.int32), TOP_K)
    buf_tok = jnp.zeros((p_rows,), jnp.int32).at[dest].set(tok)
    blk_e = jnp.minimum(jnp.searchsorted(pend, jnp.arange(n_blk, dtype=jnp.int32) * MOE_BLOCK, side="right"),
                        N_EXPERTS - 1).astype(jnp.int32)
    xs = _gather_rows(hn2, buf_tok)
    ys = _expert_mlp(xs, blk_e, w_up, w_down)
    return _gather_rows(ys, dest)


def _final_kernel(ts, x_ref, yt_ref, pg_ref, pmod_ref, gain_ref, o_ref):
    x = _residual_in(x_ref, (yt_ref, pg_ref, pmod_ref), ts)
    ms = jnp.mean(x * x, axis=-1, keepdims=True)
    o_ref[...] = x * lax.rsqrt(ms + EPS) * gain_ref[...]


def _final(x, prev, gain):
    yt, pgates, pmod = prev
    ts = TS_EVEN
    n_tok = x.shape[0]
    bsz = pmod.shape[0]
    n_s = n_tok // bsz // ts
    tok_map = lambda b, s: (b * n_s + s, 0)
    return pl.pallas_call(
        functools.partial(_final_kernel, ts),
        grid=(bsz, n_s),
        in_specs=[pl.BlockSpec((ts, D_MODEL), tok_map),
                  pl.BlockSpec((ts * TOP_K * ROW_TILES, LANES), tok_map),
                  pl.BlockSpec((ts, LANES), tok_map),
                  pl.BlockSpec((None, SUBLANES, D_MODEL), lambda b, s: (b, 0, 0)),
                  pl.BlockSpec((1, D_MODEL), lambda b, s: (0, 0))],
        out_specs=pl.BlockSpec((ts, D_MODEL), tok_map),
        out_shape=jax.ShapeDtypeStruct((n_tok, D_MODEL), F32),
        compiler_params=pltpu.CompilerParams(dimension_semantics=("arbitrary", "arbitrary"),
                                             vmem_limit_bytes=VMEM_LIMIT),
        name="final_norm",
    )(x, yt, pgates, pmod, gain.reshape(1, -1))


def kernel(x, c, mod_w, mod_b, norm_mix, norm_ffn, ab_w_in, pool_w, pool_scale, conv_w, conv_b, conv_ln_g,
           conv_ln_b, ab_w_out, dn_w_in, dn_conv_w, dn_a_log, dn_dt_bias, dn_onorm, dn_w_out, moe_w_grp,
           moe_b_grp, moe_w_exp, moe_b_exp, moe_w_up, moe_w_down, final_norm):
    bsz, seq, d = x.shape
    depth = mod_w.shape[0]
    assert d == D_MODEL and seq % TS_EVEN == 0 and seq % TS_ODD == 0
    mod = _modulation(c, mod_w, mod_b)
    xt = x.reshape(bsz * seq, d)
    prev = None
    for l in range(depth):
        i = l // 2
        w_r = jnp.pad(jnp.concatenate([moe_w_grp[l], moe_w_exp[l]], axis=1),
                      ((0, 0), (0, LANES - N_GROUPS - N_EXPERTS)))
        b_r = jnp.pad(jnp.concatenate([moe_b_grp[l], moe_b_exp[l]]), (0, LANES - N_GROUPS - N_EXPERTS))
        b_r = b_r.reshape(1, -1)
        if l % 2 == 0:
            outs = _even_layer(xt, prev, mod[l], norm_mix[l], norm_ffn[l], ab_w_in[i], pool_w[i], pool_scale[i],
                               conv_w[i], conv_b[i], conv_ln_g[i], conv_ln_b[i], ab_w_out[i], w_r, b_r)
        else:
            outs = _odd_layer(xt, prev, mod[l], norm_mix[l], norm_ffn[l], dn_w_in[i], dn_conv_w[i], dn_a_log[i],
                              dn_dt_bias[i], dn_onorm[i], dn_w_out[i], w_r, b_r)
        xt, hn2, ids, gates, counts = outs
        yt = _moe(hn2, ids, counts, moe_w_up[l].astype(BF16), moe_w_down[l].astype(BF16))
        prev = (yt, gates, mod[l])
    out = _final(xt, prev, final_norm)
    return out.reshape(bsz, seq, d)
```

```python
import functools

import jax
import jax.numpy as jnp
from jax import lax
from jax.experimental import pallas as pl
from jax.experimental.pallas import tpu as pltpu

F32 = jnp.float32
BF16 = jnp.bfloat16
HI = lax.Precision.HIGHEST

D_MODEL = 1024
EPS = 1e-6
LANES = 128
SUBLANES = 8
ROW_TILES = D_MODEL // LANES
VMEM_LIMIT = 56 * 1024 * 1024

POOL_WINDOWS = (2, 4, 8, 16)
POOL_GROUP = 128
POOL_WIDTH = 512
CONV_WIDTH = 512
CONV_K = 31
AB_IN = POOL_WIDTH + 2 * CONV_WIDTH
HALO = 32

DN_HEADS = 8
DN_DIM = 128
DN_WIDTH = DN_HEADS * DN_DIM
DN_CONV_K = 4
DN_CHUNK = 128
DN_HALO = 8
DN_GROUP = 4

N_GROUPS = 4
EPG = 8
N_EXPERTS = N_GROUPS * EPG
TOP_K = 2
D_EXPERT = 256
MOE_BLOCK = 128
ROUTER_E0 = N_GROUPS

TS_EVEN = 512
TS_ODD = 256

NEG = -1e30


def _dot(a, b):
    return jnp.dot(a, b, preferred_element_type=F32)


def _dot_nt(a, b):
    return lax.dot_general(a, b, (((1,), (1,)), ((), ())), preferred_element_type=F32)


def _dot_tn(a, b):
    return lax.dot_general(a, b, (((0,), (0,)), ((), ())), preferred_element_type=F32)


def _split_hi_lo(w):
    hi = w.astype(BF16)
    return jnp.stack([hi, (w - hi.astype(F32)).astype(BF16)])


def _dot_split(x, w_ref):
    xh = x.astype(BF16)
    xl = (x - xh.astype(F32)).astype(BF16)
    return _dot(xh, w_ref[0]) + (_dot(xl, w_ref[0]) + _dot(xh, w_ref[1]))


def _rms_mod(x, gain, scale, shift):
    ms = jnp.mean(x * x, axis=-1, keepdims=True)
    return x * lax.rsqrt(ms + EPS) * gain * (1.0 + scale) + shift


def _load_rows(ref, n, first, stride):
    return jnp.concatenate([ref[pl.ds(first + j, n, stride=stride), :] for j in range(ROW_TILES)], axis=1)


def _store_rows(ref, val, n):
    for j in range(ROW_TILES):
        ref[pl.ds(j, n, stride=ROW_TILES), :] = val[:, j * LANES:(j + 1) * LANES]


def _mod_kernel(c_ref, w_ref, b_ref, o_ref):
    ca = jax.nn.silu(c_ref[...])
    o_ref[...] = jnp.dot(ca, w_ref[...], precision=HI, preferred_element_type=F32) + b_ref[...]


def _modulation(c, mod_w, mod_b):
    depth = mod_w.shape[0]
    bsz = c.shape[0]
    n_col = mod_w.shape[2] // D_MODEL
    out = pl.pallas_call(
        _mod_kernel,
        grid=(depth, n_col),
        in_specs=[
            pl.BlockSpec((bsz, D_MODEL), lambda l, j: (0, 0)),
            pl.BlockSpec((None, D_MODEL, D_MODEL), lambda l, j: (l, 0, j)),
            pl.BlockSpec((None, 1, D_MODEL), lambda l, j: (l, 0, j)),
        ],
        out_specs=pl.BlockSpec((None, bsz, D_MODEL), lambda l, j: (l, 0, j)),
        out_shape=jax.ShapeDtypeStruct((depth, bsz, n_col * D_MODEL), F32),
        compiler_params=pltpu.CompilerParams(dimension_semantics=("arbitrary", "arbitrary"),
                                             vmem_limit_bytes=VMEM_LIMIT),
        name="modulation",
    )(c, mod_w, mod_b.reshape(depth, 1, n_col * D_MODEL))
    mod = out.reshape(depth, bsz, n_col, D_MODEL)
    return jnp.pad(mod, ((0, 0), (0, 0), (0, SUBLANES - n_col), (0, 0)))


def _residual_in(x_ref, prev, ts):
    x = x_ref[...]
    if prev is None:
        return x
    yt_ref, pg_ref, pmod_ref = prev
    y0 = _load_rows(yt_ref, ts, 0, TOP_K * ROW_TILES)
    y1 = _load_rows(yt_ref, ts, ROW_TILES, TOP_K * ROW_TILES)
    pg = pg_ref[...]
    return x + pmod_ref[5:6, :] * (y0 * pg[:, 0:1] + y1 * pg[:, 1:2])


def _ffn_prep(x_new, first_step, mod_ref, nffn_ref, wr_ref, br_ref, cnt_ref, hn2_ref, ids_ref, gates_ref,
              counts_ref, ts):
    hn2 = _rms_mod(x_new, nffn_ref[...], mod_ref[4:5, :], mod_ref[3:4, :])
    _store_rows(hn2_ref, hn2, ts)
    lg = _dot_split(hn2, wr_ref) + br_ref[...]
    lane = lax.broadcasted_iota(jnp.int32, (ts, LANES), 1)
    is_grp = lane < N_GROUPS
    gl = jnp.where(is_grp, lg, NEG)
    gmax = jnp.max(gl, axis=-1, keepdims=True)
    gsum = jnp.sum(jnp.where(is_grp, jnp.exp(gl - gmax), 0.0), axis=-1, keepdims=True)
    gp = 1.0 / gsum
    gi = jnp.min(jnp.where(gl == gmax, lane, LANES), axis=-1, keepdims=True)
    lo = ROUTER_E0 + gi * EPG
    in_grp = (lane >= lo) & (lane < lo + EPG)
    el = jnp.where(in_grp, lg, NEG)
    m1 = jnp.max(el, axis=-1, keepdims=True)
    i1 = jnp.min(jnp.where(el == m1, lane, LANES), axis=-1, keepdims=True)
    el2 = jnp.where(lane == i1, NEG, el)
    m2 = jnp.max(el2, axis=-1, keepdims=True)
    i2 = jnp.min(jnp.where(el2 == m2, lane, LANES), axis=-1, keepdims=True)
    e2 = jnp.exp(m2 - m1)
    w1 = gp / (1.0 + e2)
    w2 = gp * e2 / (1.0 + e2)

    @pl.when(first_step)
    def _():
        cnt_ref[...] = jnp.zeros_like(cnt_ref)

    onehot = ((lane == i1) | (lane == i2)).astype(F32)
    r = lax.broadcasted_iota(jnp.int32, (ts, ts), 0)
    c = lax.broadcasted_iota(jnp.int32, (ts, ts), 1)
    before = (r > c).astype(BF16)
    run = _dot(before, onehot.astype(BF16)) + cnt_ref[0:1, :]
    rank1 = jnp.sum(jnp.where(lane == i1, run, 0.0), axis=-1, keepdims=True)
    rank2 = jnp.sum(jnp.where(lane == i2, run, 0.0), axis=-1, keepdims=True)
    total = cnt_ref[0:1, :] + jnp.sum(onehot, axis=0, keepdims=True)
    cnt_ref[...] = jnp.broadcast_to(total, cnt_ref.shape)
    counts_ref[...] = jnp.broadcast_to(total, counts_ref.shape)

    ids = jnp.where(lane == 0, i1 - ROUTER_E0,
                    jnp.where(lane == 1, i2 - ROUTER_E0,
                              jnp.where(lane == 2, rank1.astype(jnp.int32),
                                        jnp.where(lane == 3, rank2.astype(jnp.int32), 0))))
    ids_ref[...] = ids
    gates_ref[...] = jnp.where(lane == 0, w1, jnp.where(lane == 1, w2, 0.0))


def _mixer_call(body, ts, x, prev, mod_l, params, scratch, name):
    n_tok = x.shape[0]
    bsz = mod_l.shape[0]
    seq = n_tok // bsz
    n_s = seq // ts
    tok_map = lambda b, s: (b * n_s + s, 0)
    in_specs = [pl.BlockSpec((ts, D_MODEL), tok_map)]
    args = [x]
    if prev is not None:
        yt, pgates, pmod = prev
        in_specs += [pl.BlockSpec((ts * TOP_K * ROW_TILES, LANES), tok_map),
                     pl.BlockSpec((ts, LANES), tok_map),
                     pl.BlockSpec((None, SUBLANES, D_MODEL), lambda b, s: (b, 0, 0))]
        args += [yt, pgates, pmod]
    in_specs.append(pl.BlockSpec((None, SUBLANES, D_MODEL), lambda b, s: (b, 0, 0)))
    args.append(mod_l)
    for arr in params:
        nd = arr.ndim
        in_specs.append(pl.BlockSpec(arr.shape, lambda b, s, _nd=nd: (0,) * _nd))
        args.append(arr)
    out_shape = [
        jax.ShapeDtypeStruct((n_tok, D_MODEL), F32),
        jax.ShapeDtypeStruct((n_tok * ROW_TILES, LANES), F32),
        jax.ShapeDtypeStruct((n_tok, LANES), jnp.int32),
        jax.ShapeDtypeStruct((n_tok, LANES), F32),
        jax.ShapeDtypeStruct((SUBLANES, LANES), F32),
    ]
    out_specs = [
        pl.BlockSpec((ts, D_MODEL), tok_map),
        pl.BlockSpec((ts * ROW_TILES, LANES), tok_map),
        pl.BlockSpec((ts, LANES), tok_map),
        pl.BlockSpec((ts, LANES), tok_map),
        pl.BlockSpec((SUBLANES, LANES), lambda b, s: (0, 0)),
    ]
    return pl.pallas_call(
        functools.partial(body, prev is not None, ts),
        grid=(bsz, n_s),
        in_specs=in_specs,
        out_specs=out_specs,
        out_shape=out_shape,
        scratch_shapes=scratch,
        compiler_params=pltpu.CompilerParams(dimension_semantics=("arbitrary", "arbitrary"),
                                             vmem_limit_bytes=VMEM_LIMIT),
        name=name,
    )(*args)


def _even_kernel(has_prev, ts, *refs):
    n_prev = 3 if has_prev else 0
    x_ref = refs[0]
    prev = refs[1:1 + n_prev] if has_prev else None
    (mod_ref, nmix_ref, nffn_ref, win_ref, poolw_ref, pscale_ref, convw_ref, convb_ref, lng_ref, lnb_ref,
     wout_ref, wr_ref, br_ref) = refs[1 + n_prev:14 + n_prev]
    xo_ref, hn2_ref, ids_ref, gates_ref, counts_ref = refs[14 + n_prev:19 + n_prev]
    extp_ref, extc_ref, shift_ref, mix_ref, cnt_ref = refs[19 + n_prev:]

    b = pl.program_id(0)
    s = pl.program_id(1)
    x_in = _residual_in(x_ref, prev, ts)
    hn = _rms_mod(x_in, nmix_ref[...], mod_ref[1:2, :], mod_ref[0:1, :])
    u = _dot(hn.astype(BF16), win_ref[...])

    @pl.when(s == 0)
    def _():
        extp_ref[0:HALO, :] = jnp.zeros((HALO, POOL_WIDTH), F32)
        extc_ref[0:HALO, :] = jnp.zeros((HALO, CONV_WIDTH), F32)

    @pl.when(s > 0)
    def _():
        extp_ref[0:HALO, :] = extp_ref[ts:ts + HALO, :]
        extc_ref[0:HALO, :] = extc_ref[ts:ts + HALO, :]

    extp_ref[HALO:HALO + ts, :] = u[:, :POOL_WIDTH]
    extc_ref[HALO:HALO + ts, :] = (u[:, POOL_WIDTH:POOL_WIDTH + CONV_WIDTH]
                                   * jax.nn.sigmoid(u[:, POOL_WIDTH + CONV_WIDTH:]))

    pos = (s * ts + 1 + lax.broadcasted_iota(jnp.int32, (ts, 1), 0)).astype(F32)
    for j, w in enumerate(POOL_WINDOWS):
        c0 = j * POOL_GROUP
        a = extp_ref[HALO:HALO + ts, c0:c0 + POOL_GROUP]
        win = a
        for d in range(1, w):
            win = win + extp_ref[HALO - d:HALO - d + ts, c0:c0 + POOL_GROUP]
        pooled = win / jnp.minimum(pos, float(w)) - a
        ya = _dot(pooled.astype(BF16), poolw_ref[j]) * pscale_ref[:, c0:c0 + POOL_GROUP]
        mix_ref[:, c0:c0 + POOL_GROUP] = ya.astype(BF16)

    for sh in range(1, SUBLANES):
        shift_ref[sh - 1, SUBLANES:HALO + ts, :] = extc_ref[SUBLANES - sh:HALO + ts - sh, :]
    rc = 32

    def conv_chunk(ci, carry):
        base = pl.multiple_of(ci * rc, rc)
        acc = jnp.broadcast_to(convb_ref[...], (rc, CONV_WIDTH))
        for k in range(CONV_K):
            delay = CONV_K - 1 - k
            a8, sh = delay // SUBLANES, delay % SUBLANES
            rows = pl.ds(base + (HALO - a8 * SUBLANES), rc)
            tap = extc_ref[rows, :] if sh == 0 else shift_ref[sh - 1, rows, :]
            acc = acc + convw_ref[k:k + 1, :] * tap
        mu = jnp.mean(acc, axis=-1, keepdims=True)
        cen = acc - mu
        var = jnp.mean(cen * cen, axis=-1, keepdims=True)
        yb = cen * lax.rsqrt(var + EPS) * lng_ref[...] + lnb_ref[...]
        mix_ref[pl.ds(base, rc), POOL_WIDTH:POOL_WIDTH + CONV_WIDTH] = jax.nn.silu(yb).astype(BF16)
        return carry

    lax.fori_loop(0, ts // rc, conv_chunk, 0)

    y = _dot(mix_ref[...], wout_ref[...])
    x_new = x_in + mod_ref[2:3, :] * y
    xo_ref[...] = x_new
    _ffn_prep(x_new, (b == 0) & (s == 0), mod_ref, nffn_ref, wr_ref, br_ref, cnt_ref, hn2_ref, ids_ref,
              gates_ref, counts_ref, ts)


def _even_layer(x, prev, mod_l, nmix, nffn, w_in, pool_w, pool_scale, conv_w, conv_b, ln_g, ln_b, w_out, w_r,
                b_r):
    ts = TS_EVEN
    params = [nmix.reshape(1, -1), nffn.reshape(1, -1), w_in.astype(BF16), pool_w.astype(BF16),
              pool_scale.reshape(1, -1), jnp.pad(conv_w, ((0, HALO - CONV_K), (0, 0))), conv_b.reshape(1, -1),
              ln_g.reshape(1, -1), ln_b.reshape(1, -1), w_out.astype(BF16), w_r, b_r]
    scratch = [pltpu.VMEM((HALO + ts, POOL_WIDTH), F32), pltpu.VMEM((HALO + ts, CONV_WIDTH), F32),
               pltpu.VMEM((SUBLANES - 1, HALO + ts, CONV_WIDTH), F32),
               pltpu.VMEM((ts, D_MODEL), BF16), pltpu.VMEM((SUBLANES, LANES), F32)]
    return _mixer_call(_even_kernel, ts, x, prev, mod_l, params, scratch, "even_mixer")


def _odd_kernel(has_prev, ts, *refs):
    n_prev = 3 if has_prev else 0
    x_ref = refs[0]
    prev = refs[1:1 + n_prev] if has_prev else None
    (mod_ref, nmix_ref, nffn_ref, win_ref, wba_ref, convw_ref, alog_ref, dtb_ref, onorm_ref, wout_ref, wr_ref,
     br_ref) = refs[1 + n_prev:13 + n_prev]
    xo_ref, hn2_ref, ids_ref, gates_ref, counts_ref = refs[13 + n_prev:18 + n_prev]
    (ext_ref, q_ref, k_ref, v_ref, z_ref, gcol_ref, bcol_ref, grow_ref, og_ref, state_ref, mix_ref,
     cnt_ref) = refs[18 + n_prev:]

    b = pl.program_id(0)
    s = pl.program_id(1)
    n_ch = ts // DN_CHUNK
    x_in = _residual_in(x_ref, prev, ts)
    hn = _rms_mod(x_in, nmix_ref[...], mod_ref[1:2, :], mod_ref[0:1, :])
    hb = hn.astype(BF16)

    @pl.when(s == 0)
    def _():
        ext_ref[0:DN_HALO, :] = jnp.zeros((DN_HALO, 3 * DN_WIDTH), F32)
        state_ref[...] = jnp.zeros_like(state_ref)

    @pl.when(s > 0)
    def _():
        ext_ref[0:DN_HALO, :] = ext_ref[ts:ts + DN_HALO, :]

    for g in range(3):
        ext_ref[DN_HALO:DN_HALO + ts, g * DN_WIDTH:(g + 1) * DN_WIDTH] = _dot(
            hb, win_ref[:, g * DN_WIDTH:(g + 1) * DN_WIDTH])
    z = _dot(hb, win_ref[:, 3 * DN_WIDTH:4 * DN_WIDTH])
    for h in range(DN_HEADS):
        z_ref[h] = z[:, h * DN_DIM:(h + 1) * DN_DIM]

    for g in range(3):
        for h in range(DN_HEADS):
            c0 = g * DN_WIDTH + h * DN_DIM
            acc = None
            for k in range(DN_CONV_K):
                off = DN_HALO - (DN_CONV_K - 1) + k
                term = convw_ref[k:k + 1, c0:c0 + DN_DIM] * ext_ref[off:off + ts, c0:c0 + DN_DIM]
                acc = term if acc is None else acc + term
            cv = jax.nn.silu(acc)
            if g == 0:
                q_ref[h] = cv * lax.rsqrt(jnp.sum(cv * cv, axis=-1, keepdims=True) + EPS) * (DN_DIM ** -0.5)
            elif g == 1:
                k_ref[h] = cv * lax.rsqrt(jnp.sum(cv * cv, axis=-1, keepdims=True) + EPS)
            else:
                v_ref[h] = cv

    ba = _dot_split(hn, wba_ref)
    beta = jax.nn.sigmoid(ba)
    gdec = -jnp.exp(alog_ref[...]) * jax.nn.softplus(ba + dtb_ref[...])
    r = lax.broadcasted_iota(jnp.int32, (ts, ts), 0)
    c = lax.broadcasted_iota(jnp.int32, (ts, ts), 1)
    same_chunk = (r // DN_CHUNK) == (c // DN_CHUNK)
    tri = ((r >= c) & same_chunk).astype(BF16)
    g_hi = gdec.astype(BF16)
    g_r1 = gdec - g_hi.astype(F32)
    g_mid = g_r1.astype(BF16)
    g_lo = (g_r1 - g_mid.astype(F32)).astype(BF16)
    gc = _dot(tri, g_hi) + (_dot(tri, g_mid) + _dot(tri, g_lo))
    gct = gc.T
    for h in range(DN_HEADS):
        gcol_ref[h] = jnp.broadcast_to(gc[:, DN_HEADS + h:DN_HEADS + h + 1], (ts, LANES))
        bcol_ref[h] = jnp.broadcast_to(beta[:, h:h + 1], (ts, LANES))
        grow_ref[h] = jnp.broadcast_to(gct[DN_HEADS + h:DN_HEADS + h + 1, :], (SUBLANES, ts))

    ri = lax.broadcasted_iota(jnp.int32, (DN_CHUNK, DN_CHUNK), 0)
    ci = lax.broadcasted_iota(jnp.int32, (DN_CHUNK, DN_CHUNK), 1)
    causal = ri >= ci
    strict = ri > ci
    eye = (ri == ci).astype(F32)

    ci_pair = lax.broadcasted_iota(jnp.int32, (DN_CHUNK, 2 * DN_CHUNK), 1) & (DN_CHUNK - 1)
    ri_pair = lax.broadcasted_iota(jnp.int32, (DN_CHUNK, 2 * DN_CHUNK), 0)
    eye_pair = (ri_pair == ci_pair).astype(F32)
    zero_blk = jnp.zeros((DN_CHUNK, DN_CHUNK), BF16)

    def block_diag(p):
        return jnp.concatenate([jnp.concatenate([p[:, :DN_CHUNK], zero_blk], axis=1),
                                jnp.concatenate([zero_blk, p[:, DN_CHUNK:]], axis=1)], axis=0)

    def chunk_prep(h, ch):
        r0 = ch * DN_CHUNK
        q = q_ref[h, r0:r0 + DN_CHUNK, :]
        k = k_ref[h, r0:r0 + DN_CHUNK, :]
        gcl = gcol_ref[h, r0:r0 + DN_CHUNK, :]
        grw = grow_ref[h, 0:1, r0:r0 + DN_CHUNK]
        bt = bcol_ref[h, r0:r0 + DN_CHUNK, :]
        decay = jnp.where(causal, jnp.exp(jnp.where(causal, gcl - grw, 0.0)), 0.0)
        kbf = k.astype(BF16)
        nl = jnp.where(strict, -(_dot_nt((k * bt).astype(BF16), kbf) * decay), 0.0)
        attn = jnp.where(causal, _dot_nt(q.astype(BF16), kbf) * decay, 0.0)
        return nl, attn.astype(BF16)

    def chunk_state(h, ch, t_c, attn, st):
        r0 = ch * DN_CHUNK
        q = q_ref[h, r0:r0 + DN_CHUNK, :]
        k = k_ref[h, r0:r0 + DN_CHUNK, :]
        v = v_ref[h, r0:r0 + DN_CHUNK, :]
        gcl = gcol_ref[h, r0:r0 + DN_CHUNK, :]
        bt = bcol_ref[h, r0:r0 + DN_CHUNK, :]
        eg = jnp.exp(gcl)
        rhs = jnp.concatenate([v * bt, k * bt * eg], axis=1).astype(BF16)
        uw = _dot(t_c.astype(BF16), rhs)
        glast = gcl[DN_CHUNK - 1:DN_CHUNK, :]
        sb = st.astype(BF16)
        v_new = uw[:, :DN_DIM] - _dot(uw[:, DN_DIM:].astype(BF16), sb)
        vnb = v_new.astype(BF16)
        o = _dot((q * eg).astype(BF16), sb) + _dot(attn, vnb)
        st = st * jnp.exp(glast) + _dot_tn((k * jnp.exp(glast - gcl)).astype(BF16), vnb)
        zz = z_ref[h, r0:r0 + DN_CHUNK, :]
        o = o * lax.rsqrt(jnp.mean(o * o, axis=-1, keepdims=True) + EPS) * onorm_ref[...] * jax.nn.silu(zz)
        og_ref[h, r0:r0 + DN_CHUNK, :] = o
        return st

    def group_body(hg, carry):
        heads = [hg * DN_GROUP + g for g in range(DN_GROUP)]
        sts = [state_ref[h] for h in heads]
        for ch0 in range(0, n_ch, 2):
            preps = [(chunk_prep(h, ch0), chunk_prep(h, ch0 + 1)) for h in heads]
            nl2s = [jnp.concatenate([p[0][0], p[1][0]], axis=1) for p in preps]
            tms = [eye_pair + jnp.where((ri_pair >> 1) == (ci_pair >> 1), nl2, 0.0) for nl2 in nl2s]
            blk = 2
            while blk < DN_CHUNK:
                sel = (((ri_pair // (2 * blk)) == (ci_pair // (2 * blk)))
                       & ((ri_pair & (2 * blk - 1)) >= blk) & ((ci_pair & (2 * blk - 1)) < blk))
                tmbs = [tm.astype(BF16) for tm in tms]
                ys = [_dot(jnp.where(sel, nl2, 0.0).astype(BF16), block_diag(tmb))
                      for nl2, tmb in zip(nl2s, tmbs)]
                tms = [tm + _dot(tmb, block_diag(y.astype(BF16))) for tm, tmb, y in zip(tms, tmbs, ys)]
                blk *= 2
            for idx in range(2):
                sts = [chunk_state(h, ch0 + idx, tm[:, idx * DN_CHUNK:(idx + 1) * DN_CHUNK], p[idx][1], st)
                       for h, tm, p, st in zip(heads, tms, preps, sts)]
        for h, st in zip(heads, sts):
            state_ref[h] = st
        return carry

    lax.fori_loop(0, DN_HEADS // DN_GROUP, group_body, 0)

    for h in range(DN_HEADS):
        mix_ref[:, h * DN_DIM:(h + 1) * DN_DIM] = og_ref[h].astype(BF16)
    y = _dot(mix_ref[...], wout_ref[...])
    x_new = x_in + mod_ref[2:3, :] * y
    xo_ref[...] = x_new
    _ffn_prep(x_new, (b == 0) & (s == 0), mod_ref, nffn_ref, wr_ref, br_ref, cnt_ref, hn2_ref, ids_ref,
              gates_ref, counts_ref, ts)


def _odd_layer(x, prev, mod_l, nmix, nffn, w_in, conv_w, a_log, dt_bias, onorm, w_out, w_r, b_r):
    ts = TS_ODD
    w_main = w_in[:, :4 * DN_WIDTH].astype(BF16)
    w_ba = _split_hi_lo(jnp.pad(w_in[:, 4 * DN_WIDTH:], ((0, 0), (0, LANES - 2 * DN_HEADS))))
    lane_pad = (DN_HEADS, LANES - 2 * DN_HEADS)
    params = [nmix.reshape(1, -1), nffn.reshape(1, -1), w_main, w_ba,
              jnp.pad(conv_w, ((0, SUBLANES - DN_CONV_K), (0, 0))),
              jnp.pad(a_log, lane_pad).reshape(1, -1), jnp.pad(dt_bias, lane_pad).reshape(1, -1),
              onorm.reshape(1, -1), w_out.astype(BF16), w_r, b_r]
    head = (DN_HEADS, ts, DN_DIM)
    scratch = [pltpu.VMEM((DN_HALO + ts, 3 * DN_WIDTH), F32),
               pltpu.VMEM(head, F32), pltpu.VMEM(head, F32), pltpu.VMEM(head, F32), pltpu.VMEM(head, F32),
               pltpu.VMEM(head, F32), pltpu.VMEM(head, F32), pltpu.VMEM((DN_HEADS, SUBLANES, ts), F32),
               pltpu.VMEM(head, F32), pltpu.VMEM((DN_HEADS, DN_DIM, DN_DIM), F32),
               pltpu.VMEM((ts, D_MODEL), BF16), pltpu.VMEM((SUBLANES, LANES), F32)]
    return _mixer_call(_odd_kernel, ts, x, prev, mod_l, params, scratch, "deltanet_mixer")


def _expert_kernel(blk_e_ref, tok_ref, tok_next_ref, slot_ref, hn2_ref, wup_ref, wdn_ref, yt_ref,
                   xg_ref, ys_ref, xb_ref, sem_in, sem_out):
    i = pl.program_id(0)
    n = pl.num_programs(0)
    cur = i % 2

    def gather_copy(tok, r, buf):
        src = hn2_ref.at[pl.ds(pl.multiple_of(tok * ROW_TILES, ROW_TILES), ROW_TILES), :]
        return pltpu.make_async_copy(src, xg_ref.at[buf, pl.ds(r * ROW_TILES, ROW_TILES), :], sem_in.at[buf])

    def scatter_copy(slot, r, buf):
        dst = yt_ref.at[pl.ds(pl.multiple_of(slot * ROW_TILES, ROW_TILES), ROW_TILES), :]
        return pltpu.make_async_copy(ys_ref.at[buf, pl.ds(r * ROW_TILES, ROW_TILES), :], dst, sem_out.at[buf])

    block_rows = pl.ds(0, MOE_BLOCK * ROW_TILES)

    def wait_gathers(buf):
        pltpu.make_async_copy(hn2_ref.at[block_rows, :], xg_ref.at[buf], sem_in.at[buf]).wait()

    def wait_scatters(buf):
        pltpu.make_async_copy(ys_ref.at[buf], yt_ref.at[block_rows, :], sem_out.at[buf]).wait()

    @pl.when(i == 0)
    def _():
        for r in range(MOE_BLOCK):
            gather_copy(tok_ref[0, r], r, 0).start()

    @pl.when(i + 1 < n)
    def _():
        for r in range(MOE_BLOCK):
            gather_copy(tok_next_ref[0, r], r, 1 - cur).start()

    wait_gathers(cur)
    for j in range(ROW_TILES):
        xb_ref[:, j * LANES:(j + 1) * LANES] = xg_ref[cur, pl.ds(j, MOE_BLOCK, stride=ROW_TILES), :].astype(BF16)
    gu = _dot(xb_ref[...], wup_ref[...])
    hid = jax.nn.silu(gu[:, :D_EXPERT]) * gu[:, D_EXPERT:]
    y = _dot(hid.astype(BF16), wdn_ref[...])

    @pl.when(i >= 2)
    def _():
        wait_scatters(cur)

    for j in range(ROW_TILES):
        ys_ref[cur, pl.ds(j, MOE_BLOCK, stride=ROW_TILES), :] = y[:, j * LANES:(j + 1) * LANES]
    for r in range(MOE_BLOCK):
        scatter_copy(slot_ref[0, r], r, cur).start()

    @pl.when(i == n - 1)
    def _():
        wait_scatters(cur)

        @pl.when(n >= 2)
        def _():
            wait_scatters(1 - cur)


def _expert_mlp(hn2, buf_tok, buf_slot, blk_e, n_slots, w_up, w_down):
    n_blk = blk_e.shape[0]
    idx_block = (None, 1, MOE_BLOCK)
    grid_spec = pltpu.PrefetchScalarGridSpec(
        num_scalar_prefetch=1,
        grid=(n_blk,),
        in_specs=[pl.BlockSpec(idx_block, lambda i, be: (i, 0, 0), memory_space=pltpu.SMEM),
                  pl.BlockSpec(idx_block, lambda i, be: (jnp.minimum(i + 1, n_blk - 1), 0, 0),
                               memory_space=pltpu.SMEM),
                  pl.BlockSpec(idx_block, lambda i, be: (i, 0, 0), memory_space=pltpu.SMEM),
                  pl.BlockSpec(memory_space=pl.ANY),
                  pl.BlockSpec((None, D_MODEL, 2 * D_EXPERT), lambda i, be: (be[i], 0, 0)),
                  pl.BlockSpec((None, D_EXPERT, D_MODEL), lambda i, be: (be[i], 0, 0))],
        out_specs=pl.BlockSpec(memory_space=pl.ANY),
        scratch_shapes=[pltpu.VMEM((2, MOE_BLOCK * ROW_TILES, LANES), F32),
                        pltpu.VMEM((2, MOE_BLOCK * ROW_TILES, LANES), F32),
                        pltpu.VMEM((MOE_BLOCK, D_MODEL), BF16),
                        pltpu.SemaphoreType.DMA((2,)), pltpu.SemaphoreType.DMA((2,))],
    )
    tok3 = buf_tok.reshape(n_blk, 1, MOE_BLOCK)
    return pl.pallas_call(
        _expert_kernel,
        grid_spec=grid_spec,
        out_shape=jax.ShapeDtypeStruct((n_slots * ROW_TILES, LANES), F32),
        compiler_params=pltpu.CompilerParams(dimension_semantics=("arbitrary",), vmem_limit_bytes=VMEM_LIMIT),
        name="expert_mlp",
    )(blk_e, tok3, tok3, buf_slot.reshape(n_blk, 1, MOE_BLOCK), hn2, w_up, w_down)


def _moe(hn2, ids, counts, w_up, w_down):
    n_tok = ids.shape[0]
    m = n_tok * TOP_K
    p_rows = (m + N_EXPERTS * (MOE_BLOCK - 1) + MOE_BLOCK - 1) // MOE_BLOCK * MOE_BLOCK
    n_blk = p_rows // MOE_BLOCK
    eid = ids[:, 0:TOP_K]
    rank = ids[:, TOP_K:2 * TOP_K]
    cnt = counts[0, ROUTER_E0:ROUTER_E0 + N_EXPERTS].astype(jnp.int32)
    padded = (cnt + MOE_BLOCK - 1) // MOE_BLOCK * MOE_BLOCK
    pend = jnp.cumsum(padded)
    pstart = pend - padded
    dest = (pstart[eid] + rank).reshape(-1)
    slot_of_row = jnp.full((p_rows,), -1, jnp.int32).at[dest].set(jnp.arange(m, dtype=jnp.int32))
    is_pad = slot_of_row < 0
    pad_rank = jnp.cumsum(is_pad.astype(jnp.int32)) - 1
    buf_slot = jnp.where(is_pad, m + pad_rank, slot_of_row)
    buf_tok = jnp.where(is_pad, 0, slot_of_row // TOP_K)
    blk_start = jnp.arange(n_blk, dtype=jnp.int32) * MOE_BLOCK
    blk_e = jnp.minimum(jnp.sum(pend[None, :] <= blk_start[:, None], axis=1), N_EXPERTS - 1).astype(jnp.int32)
    return _expert_mlp(hn2, buf_tok, buf_slot, blk_e, p_rows, w_up, w_down)


def _final_kernel(ts, x_ref, yt_ref, pg_ref, pmod_ref, gain_ref, o_ref):
    x = _residual_in(x_ref, (yt_ref, pg_ref, pmod_ref), ts)
    ms = jnp.mean(x * x, axis=-1, keepdims=True)
    o_ref[...] = x * lax.rsqrt(ms + EPS) * gain_ref[...]


def _final(x, prev, gain):
    yt, pgates, pmod = prev
    ts = TS_EVEN
    n_tok = x.shape[0]
    bsz = pmod.shape[0]
    n_s = n_tok // bsz // ts
    tok_map = lambda b, s: (b * n_s + s, 0)
    return pl.pallas_call(
        functools.partial(_final_kernel, ts),
        grid=(bsz, n_s),
        in_specs=[pl.BlockSpec((ts, D_MODEL), tok_map),
                  pl.BlockSpec((ts * TOP_K * ROW_TILES, LANES), tok_map),
                  pl.BlockSpec((ts, LANES), tok_map),
                  pl.BlockSpec((None, SUBLANES, D_MODEL), lambda b, s: (b, 0, 0)),
                  pl.BlockSpec((1, D_MODEL), lambda b, s: (0, 0))],
        out_specs=pl.BlockSpec((ts, D_MODEL), tok_map),
        out_shape=jax.ShapeDtypeStruct((n_tok, D_MODEL), F32),
        compiler_params=pltpu.CompilerParams(dimension_semantics=("arbitrary", "arbitrary"),
                                             vmem_limit_bytes=VMEM_LIMIT),
        name="final_norm",
    )(x, yt, pgates, pmod, gain.reshape(1, -1))


def kernel(x, c, mod_w, mod_b, norm_mix, norm_ffn, ab_w_in, pool_w, pool_scale, conv_w, conv_b, conv_ln_g,
           conv_ln_b, ab_w_out, dn_w_in, dn_conv_w, dn_a_log, dn_dt_bias, dn_onorm, dn_w_out, moe_w_grp,
           moe_b_grp, moe_w_exp, moe_b_exp, moe_w_up, moe_w_down, final_norm):
    bsz, seq, d = x.shape
    depth = mod_w.shape[0]
    assert d == D_MODEL and seq % TS_EVEN == 0 and seq % TS_ODD == 0
    mod = _modulation(c, mod_w, mod_b)
    xt = x.reshape(bsz * seq, d)
    prev = None
    for l in range(depth):
        i = l // 2
        w_r = _split_hi_lo(jnp.pad(jnp.concatenate([moe_w_grp[l], moe_w_exp[l]], axis=1),
                                   ((0, 0), (0, LANES - N_GROUPS - N_EXPERTS))))
        b_r = jnp.pad(jnp.concatenate([moe_b_grp[l], moe_b_exp[l]]), (0, LANES - N_GROUPS - N_EXPERTS))
        b_r = b_r.reshape(1, -1)
        if l % 2 == 0:
            outs = _even_layer(xt, prev, mod[l], norm_mix[l], norm_ffn[l], ab_w_in[i], pool_w[i], pool_scale[i],
                               conv_w[i], conv_b[i], conv_ln_g[i], conv_ln_b[i], ab_w_out[i], w_r, b_r)
        else:
            outs = _odd_layer(xt, prev, mod[l], norm_mix[l], norm_ffn[l], dn_w_in[i], dn_conv_w[i], dn_a_log[i],
                              dn_dt_bias[i], dn_onorm[i], dn_w_out[i], w_r, b_r)
        xt, hn2, ids, gates, counts = outs
        yt = _moe(hn2, ids, counts, moe_w_up[l].astype(BF16), moe_w_down[l].astype(BF16))
        prev = (yt, gates, mod[l])
    out = _final(xt, prev, final_norm)
    return out.reshape(bsz, seq, d)
```

```python
import functools

import jax
import jax.numpy as jnp
from jax import lax
from jax.experimental import pallas as pl
from jax.experimental.pallas import tpu as pltpu

F32 = jnp.float32
BF16 = jnp.bfloat16
HI = lax.Precision.HIGHEST

D_MODEL = 1024
EPS = 1e-6
LANES = 128
SUBLANES = 8
ROW_TILES = D_MODEL // LANES
VMEM_LIMIT = 56 * 1024 * 1024

POOL_WINDOWS = (2, 4, 8, 16)
POOL_GROUP = 128
POOL_WIDTH = 512
CONV_WIDTH = 512
CONV_K = 31
AB_IN = POOL_WIDTH + 2 * CONV_WIDTH
HALO = 32

DN_HEADS = 8
DN_DIM = 128
DN_WIDTH = DN_HEADS * DN_DIM
DN_CONV_K = 4
DN_CHUNK = 128
DN_HALO = 8
DN_GROUP = 8

N_GROUPS = 4
EPG = 8
N_EXPERTS = N_GROUPS * EPG
TOP_K = 2
D_EXPERT = 256
MOE_BLOCK = 128
ROUTER_E0 = N_GROUPS

TS_EVEN = 512
TS_ODD = 256

NEG = -1e30


def _dot(a, b):
    return jnp.dot(a, b, preferred_element_type=F32)


def _dot_nt(a, b):
    return lax.dot_general(a, b, (((1,), (1,)), ((), ())), preferred_element_type=F32)


def _dot_tn(a, b):
    return lax.dot_general(a, b, (((0,), (0,)), ((), ())), preferred_element_type=F32)


def _split_hi_lo(w):
    hi = w.astype(BF16)
    return jnp.stack([hi, (w - hi.astype(F32)).astype(BF16)])


def _dot_split(x, w_ref):
    xh = x.astype(BF16)
    xl = (x - xh.astype(F32)).astype(BF16)
    return _dot(xh, w_ref[0]) + (_dot(xl, w_ref[0]) + _dot(xh, w_ref[1]))


def _rms_mod(x, gain, scale, shift):
    ms = jnp.mean(x * x, axis=-1, keepdims=True)
    return x * lax.rsqrt(ms + EPS) * gain * (1.0 + scale) + shift


def _load_rows(ref, n, first, stride):
    return jnp.concatenate([ref[pl.ds(first + j, n, stride=stride), :] for j in range(ROW_TILES)], axis=1)


def _store_rows(ref, val, n):
    for j in range(ROW_TILES):
        ref[pl.ds(j, n, stride=ROW_TILES), :] = val[:, j * LANES:(j + 1) * LANES]


def _mod_kernel(c_ref, w_ref, b_ref, o_ref):
    ca = jax.nn.silu(c_ref[...])
    o_ref[...] = jnp.dot(ca, w_ref[...], precision=HI, preferred_element_type=F32) + b_ref[...]


def _modulation(c, mod_w, mod_b):
    depth = mod_w.shape[0]
    bsz = c.shape[0]
    n_col = mod_w.shape[2] // D_MODEL
    out = pl.pallas_call(
        _mod_kernel,
        grid=(depth, n_col),
        in_specs=[
            pl.BlockSpec((bsz, D_MODEL), lambda l, j: (0, 0)),
            pl.BlockSpec((None, D_MODEL, D_MODEL), lambda l, j: (l, 0, j)),
            pl.BlockSpec((None, 1, D_MODEL), lambda l, j: (l, 0, j)),
        ],
        out_specs=pl.BlockSpec((None, bsz, D_MODEL), lambda l, j: (l, 0, j)),
        out_shape=jax.ShapeDtypeStruct((depth, bsz, n_col * D_MODEL), F32),
        compiler_params=pltpu.CompilerParams(dimension_semantics=("arbitrary", "arbitrary"),
                                             vmem_limit_bytes=VMEM_LIMIT),
        name="modulation",
    )(c, mod_w, mod_b.reshape(depth, 1, n_col * D_MODEL))
    mod = out.reshape(depth, bsz, n_col, D_MODEL)
    return jnp.pad(mod, ((0, 0), (0, 0), (0, SUBLANES - n_col), (0, 0)))


def _residual_in(x_ref, prev, ts):
    x = x_ref[...]
    if prev is None:
        return x
    yt_ref, pg_ref, pmod_ref = prev
    y0 = _load_rows(yt_ref, ts, 0, TOP_K * ROW_TILES)
    y1 = _load_rows(yt_ref, ts, ROW_TILES, TOP_K * ROW_TILES)
    pg = pg_ref[...]
    return x + pmod_ref[5:6, :] * (y0 * pg[:, 0:1] + y1 * pg[:, 1:2])


def _ffn_prep(x_new, first_step, mod_ref, nffn_ref, wr_ref, br_ref, cnt_ref, hn2_ref, ids_ref, gates_ref,
              counts_ref, ts):
    hn2 = _rms_mod(x_new, nffn_ref[...], mod_ref[4:5, :], mod_ref[3:4, :])
    _store_rows(hn2_ref, hn2, ts)
    lg = _dot_split(hn2, wr_ref) + br_ref[...]
    lane = lax.broadcasted_iota(jnp.int32, (ts, LANES), 1)
    is_grp = lane < N_GROUPS
    gl = jnp.where(is_grp, lg, NEG)
    gmax = jnp.max(gl, axis=-1, keepdims=True)
    gsum = jnp.sum(jnp.where(is_grp, jnp.exp(gl - gmax), 0.0), axis=-1, keepdims=True)
    gp = 1.0 / gsum
    gi = jnp.min(jnp.where(gl == gmax, lane, LANES), axis=-1, keepdims=True)
    lo = ROUTER_E0 + gi * EPG
    in_grp = (lane >= lo) & (lane < lo + EPG)
    el = jnp.where(in_grp, lg, NEG)
    m1 = jnp.max(el, axis=-1, keepdims=True)
    i1 = jnp.min(jnp.where(el == m1, lane, LANES), axis=-1, keepdims=True)
    el2 = jnp.where(lane == i1, NEG, el)
    m2 = jnp.max(el2, axis=-1, keepdims=True)
    i2 = jnp.min(jnp.where(el2 == m2, lane, LANES), axis=-1, keepdims=True)
    e2 = jnp.exp(m2 - m1)
    w1 = gp / (1.0 + e2)
    w2 = gp * e2 / (1.0 + e2)

    @pl.when(first_step)
    def _():
        cnt_ref[...] = jnp.zeros_like(cnt_ref)

    onehot = ((lane == i1) | (lane == i2)).astype(F32)
    r = lax.broadcasted_iota(jnp.int32, (ts, ts), 0)
    c = lax.broadcasted_iota(jnp.int32, (ts, ts), 1)
    before = (r > c).astype(BF16)
    run = _dot(before, onehot.astype(BF16)) + cnt_ref[0:1, :]
    rank1 = jnp.sum(jnp.where(lane == i1, run, 0.0), axis=-1, keepdims=True)
    rank2 = jnp.sum(jnp.where(lane == i2, run, 0.0), axis=-1, keepdims=True)
    total = cnt_ref[0:1, :] + jnp.sum(onehot, axis=0, keepdims=True)
    cnt_ref[...] = jnp.broadcast_to(total, cnt_ref.shape)
    counts_ref[...] = jnp.broadcast_to(total, counts_ref.shape)

    ids = jnp.where(lane == 0, i1 - ROUTER_E0,
                    jnp.where(lane == 1, i2 - ROUTER_E0,
                              jnp.where(lane == 2, rank1.astype(jnp.int32),
                                        jnp.where(lane == 3, rank2.astype(jnp.int32), 0))))
    ids_ref[...] = ids
    gates_ref[...] = jnp.where(lane == 0, w1, jnp.where(lane == 1, w2, 0.0))


def _mixer_call(body, ts, x, prev, mod_l, params, scratch, name):
    n_tok = x.shape[0]
    bsz = mod_l.shape[0]
    seq = n_tok // bsz
    n_s = seq // ts
    tok_map = lambda b, s: (b * n_s + s, 0)
    in_specs = [pl.BlockSpec((ts, D_MODEL), tok_map)]
    args = [x]
    if prev is not None:
        yt, pgates, pmod = prev
        in_specs += [pl.BlockSpec((ts * TOP_K * ROW_TILES, LANES), tok_map),
                     pl.BlockSpec((ts, LANES), tok_map),
                     pl.BlockSpec((None, SUBLANES, D_MODEL), lambda b, s: (b, 0, 0))]
        args += [yt, pgates, pmod]
    in_specs.append(pl.BlockSpec((None, SUBLANES, D_MODEL), lambda b, s: (b, 0, 0)))
    args.append(mod_l)
    for arr in params:
        nd = arr.ndim
        in_specs.append(pl.BlockSpec(arr.shape, lambda b, s, _nd=nd: (0,) * _nd))
        args.append(arr)
    out_shape = [
        jax.ShapeDtypeStruct((n_tok, D_MODEL), F32),
        jax.ShapeDtypeStruct((n_tok * ROW_TILES, LANES), F32),
        jax.ShapeDtypeStruct((n_tok, LANES), jnp.int32),
        jax.ShapeDtypeStruct((n_tok, LANES), F32),
        jax.ShapeDtypeStruct((SUBLANES, LANES), F32),
    ]
    out_specs = [
        pl.BlockSpec((ts, D_MODEL), tok_map),
        pl.BlockSpec((ts * ROW_TILES, LANES), tok_map),
        pl.BlockSpec((ts, LANES), tok_map),
        pl.BlockSpec((ts, LANES), tok_map),
        pl.BlockSpec((SUBLANES, LANES), lambda b, s: (0, 0)),
    ]
    return pl.pallas_call(
        functools.partial(body, prev is not None, ts),
        grid=(bsz, n_s),
        in_specs=in_specs,
        out_specs=out_specs,
        out_shape=out_shape,
        scratch_shapes=scratch,
        compiler_params=pltpu.CompilerParams(dimension_semantics=("arbitrary", "arbitrary"),
                                             vmem_limit_bytes=VMEM_LIMIT),
        name=name,
    )(*args)


def _even_kernel(has_prev, ts, *refs):
    n_prev = 3 if has_prev else 0
    x_ref = refs[0]
    prev = refs[1:1 + n_prev] if has_prev else None
    (mod_ref, nmix_ref, nffn_ref, win_ref, poolw_ref, pscale_ref, convw_ref, convb_ref, lng_ref, lnb_ref,
     wout_ref, wr_ref, br_ref) = refs[1 + n_prev:14 + n_prev]
    xo_ref, hn2_ref, ids_ref, gates_ref, counts_ref = refs[14 + n_prev:19 + n_prev]
    extp_ref, extc_ref, shift_ref, mix_ref, cnt_ref = refs[19 + n_prev:]

    b = pl.program_id(0)
    s = pl.program_id(1)
    x_in = _residual_in(x_ref, prev, ts)
    hn = _rms_mod(x_in, nmix_ref[...], mod_ref[1:2, :], mod_ref[0:1, :])
    u = _dot(hn.astype(BF16), win_ref[...])

    @pl.when(s == 0)
    def _():
        extp_ref[0:HALO, :] = jnp.zeros((HALO, POOL_WIDTH), F32)
        extc_ref[0:HALO, :] = jnp.zeros((HALO, CONV_WIDTH), F32)

    @pl.when(s > 0)
    def _():
        extp_ref[0:HALO, :] = extp_ref[ts:ts + HALO, :]
        extc_ref[0:HALO, :] = extc_ref[ts:ts + HALO, :]

    extp_ref[HALO:HALO + ts, :] = u[:, :POOL_WIDTH]
    extc_ref[HALO:HALO + ts, :] = (u[:, POOL_WIDTH:POOL_WIDTH + CONV_WIDTH]
                                   * jax.nn.sigmoid(u[:, POOL_WIDTH + CONV_WIDTH:]))

    pos = (s * ts + 1 + lax.broadcasted_iota(jnp.int32, (ts, 1), 0)).astype(F32)
    for j, w in enumerate(POOL_WINDOWS):
        c0 = j * POOL_GROUP
        a = extp_ref[HALO:HALO + ts, c0:c0 + POOL_GROUP]
        win = a
        for d in range(1, w):
            win = win + extp_ref[HALO - d:HALO - d + ts, c0:c0 + POOL_GROUP]
        pooled = win / jnp.minimum(pos, float(w)) - a
        ya = _dot(pooled.astype(BF16), poolw_ref[j]) * pscale_ref[:, c0:c0 + POOL_GROUP]
        mix_ref[:, c0:c0 + POOL_GROUP] = ya.astype(BF16)

    for sh in range(1, SUBLANES):
        shift_ref[sh - 1, SUBLANES:HALO + ts, :] = extc_ref[SUBLANES - sh:HALO + ts - sh, :]
    rc = 32

    def conv_chunk(ci, carry):
        base = pl.multiple_of(ci * rc, rc)
        acc = jnp.broadcast_to(convb_ref[...], (rc, CONV_WIDTH))
        for k in range(CONV_K):
            delay = CONV_K - 1 - k
            a8, sh = delay // SUBLANES, delay % SUBLANES
            rows = pl.ds(base + (HALO - a8 * SUBLANES), rc)
            tap = extc_ref[rows, :] if sh == 0 else shift_ref[sh - 1, rows, :]
            acc = acc + convw_ref[k:k + 1, :] * tap
        mu = jnp.mean(acc, axis=-1, keepdims=True)
        cen = acc - mu
        var = jnp.mean(cen * cen, axis=-1, keepdims=True)
        yb = cen * lax.rsqrt(var + EPS) * lng_ref[...] + lnb_ref[...]
        mix_ref[pl.ds(base, rc), POOL_WIDTH:POOL_WIDTH + CONV_WIDTH] = jax.nn.silu(yb).astype(BF16)
        return carry

    lax.fori_loop(0, ts // rc, conv_chunk, 0)

    y = _dot(mix_ref[...], wout_ref[...])
    x_new = x_in + mod_ref[2:3, :] * y
    xo_ref[...] = x_new
    _ffn_prep(x_new, (b == 0) & (s == 0), mod_ref, nffn_ref, wr_ref, br_ref, cnt_ref, hn2_ref, ids_ref,
              gates_ref, counts_ref, ts)


def _even_layer(x, prev, mod_l, nmix, nffn, w_in, pool_w, pool_scale, conv_w, conv_b, ln_g, ln_b, w_out, w_r,
                b_r):
    ts = TS_EVEN
    params = [nmix.reshape(1, -1), nffn.reshape(1, -1), w_in.astype(BF16), pool_w.astype(BF16),
              pool_scale.reshape(1, -1), jnp.pad(conv_w, ((0, HALO - CONV_K), (0, 0))), conv_b.reshape(1, -1),
              ln_g.reshape(1, -1), ln_b.reshape(1, -1), w_out.astype(BF16), w_r, b_r]
    scratch = [pltpu.VMEM((HALO + ts, POOL_WIDTH), F32), pltpu.VMEM((HALO + ts, CONV_WIDTH), F32),
               pltpu.VMEM((SUBLANES - 1, HALO + ts, CONV_WIDTH), F32),
               pltpu.VMEM((ts, D_MODEL), BF16), pltpu.VMEM((SUBLANES, LANES), F32)]
    return _mixer_call(_even_kernel, ts, x, prev, mod_l, params, scratch, "even_mixer")


def _odd_kernel(has_prev, ts, *refs):
    n_prev = 3 if has_prev else 0
    x_ref = refs[0]
    prev = refs[1:1 + n_prev] if has_prev else None
    (mod_ref, nmix_ref, nffn_ref, win_ref, wba_ref, convw_ref, alog_ref, dtb_ref, onorm_ref, wout_ref, wr_ref,
     br_ref) = refs[1 + n_prev:13 + n_prev]
    xo_ref, hn2_ref, ids_ref, gates_ref, counts_ref = refs[13 + n_prev:18 + n_prev]
    (ext_ref, q_ref, k_ref, v_ref, z_ref, gcol_ref, bcol_ref, grow_ref, og_ref, state_ref, mix_ref,
     cnt_ref) = refs[18 + n_prev:]

    b = pl.program_id(0)
    s = pl.program_id(1)
    n_ch = ts // DN_CHUNK
    x_in = _residual_in(x_ref, prev, ts)
    hn = _rms_mod(x_in, nmix_ref[...], mod_ref[1:2, :], mod_ref[0:1, :])
    hb = hn.astype(BF16)

    @pl.when(s == 0)
    def _():
        ext_ref[0:DN_HALO, :] = jnp.zeros((DN_HALO, 3 * DN_WIDTH), F32)
        state_ref[...] = jnp.zeros_like(state_ref)

    @pl.when(s > 0)
    def _():
        ext_ref[0:DN_HALO, :] = ext_ref[ts:ts + DN_HALO, :]

    for g in range(3):
        ext_ref[DN_HALO:DN_HALO + ts, g * DN_WIDTH:(g + 1) * DN_WIDTH] = _dot(
            hb, win_ref[:, g * DN_WIDTH:(g + 1) * DN_WIDTH])
    z = _dot(hb, win_ref[:, 3 * DN_WIDTH:4 * DN_WIDTH])
    for h in range(DN_HEADS):
        z_ref[h] = z[:, h * DN_DIM:(h + 1) * DN_DIM]

    for g in range(3):
        for h in range(DN_HEADS):
            c0 = g * DN_WIDTH + h * DN_DIM
            acc = None
            for k in range(DN_CONV_K):
                off = DN_HALO - (DN_CONV_K - 1) + k
                term = convw_ref[k:k + 1, c0:c0 + DN_DIM] * ext_ref[off:off + ts, c0:c0 + DN_DIM]
                acc = term if acc is None else acc + term
            cv = jax.nn.silu(acc)
            if g == 0:
                q_ref[h] = cv * lax.rsqrt(jnp.sum(cv * cv, axis=-1, keepdims=True) + EPS) * (DN_DIM ** -0.5)
            elif g == 1:
                k_ref[h] = cv * lax.rsqrt(jnp.sum(cv * cv, axis=-1, keepdims=True) + EPS)
            else:
                v_ref[h] = cv

    ba = _dot_split(hn, wba_ref)
    beta = jax.nn.sigmoid(ba)
    gdec = -jnp.exp(alog_ref[...]) * jax.nn.softplus(ba + dtb_ref[...])
    r = lax.broadcasted_iota(jnp.int32, (ts, ts), 0)
    c = lax.broadcasted_iota(jnp.int32, (ts, ts), 1)
    same_chunk = (r // DN_CHUNK) == (c // DN_CHUNK)
    tri = ((r >= c) & same_chunk).astype(BF16)
    g_hi = gdec.astype(BF16)
    g_r1 = gdec - g_hi.astype(F32)
    g_mid = g_r1.astype(BF16)
    g_lo = (g_r1 - g_mid.astype(F32)).astype(BF16)
    gc = _dot(tri, g_hi) + (_dot(tri, g_mid) + _dot(tri, g_lo))
    gct = gc.T
    for h in range(DN_HEADS):
        gcol_ref[h] = jnp.broadcast_to(gc[:, DN_HEADS + h:DN_HEADS + h + 1], (ts, LANES))
        bcol_ref[h] = jnp.broadcast_to(beta[:, h:h + 1], (ts, LANES))
        grow_ref[h] = jnp.broadcast_to(gct[DN_HEADS + h:DN_HEADS + h + 1, :], (SUBLANES, ts))

    ri = lax.broadcasted_iota(jnp.int32, (DN_CHUNK, DN_CHUNK), 0)
    ci = lax.broadcasted_iota(jnp.int32, (DN_CHUNK, DN_CHUNK), 1)
    causal = ri >= ci
    strict = ri > ci
    eye = (ri == ci).astype(F32)

    ci_pair = lax.broadcasted_iota(jnp.int32, (DN_CHUNK, 2 * DN_CHUNK), 1) & (DN_CHUNK - 1)
    ri_pair = lax.broadcasted_iota(jnp.int32, (DN_CHUNK, 2 * DN_CHUNK), 0)
    eye_pair = (ri_pair == ci_pair).astype(F32)
    zero_blk = jnp.zeros((DN_CHUNK, DN_CHUNK), BF16)

    def block_diag(p):
        return jnp.concatenate([jnp.concatenate([p[:, :DN_CHUNK], zero_blk], axis=1),
                                jnp.concatenate([zero_blk, p[:, DN_CHUNK:]], axis=1)], axis=0)

    def chunk_prep(h, ch):
        r0 = ch * DN_CHUNK
        q = q_ref[h, r0:r0 + DN_CHUNK, :]
        k = k_ref[h, r0:r0 + DN_CHUNK, :]
        gcl = gcol_ref[h, r0:r0 + DN_CHUNK, :]
        grw = grow_ref[h, 0:1, r0:r0 + DN_CHUNK]
        bt = bcol_ref[h, r0:r0 + DN_CHUNK, :]
        decay = jnp.where(causal, jnp.exp(jnp.where(causal, gcl - grw, 0.0)), 0.0)
        kbf = k.astype(BF16)
        nl = jnp.where(strict, -(_dot_nt((k * bt).astype(BF16), kbf) * decay), 0.0)
        attn = jnp.where(causal, _dot_nt(q.astype(BF16), kbf) * decay, 0.0)
        return nl, attn.astype(BF16)

    def chunk_state(h, ch, t_c, attn, st):
        r0 = ch * DN_CHUNK
        q = q_ref[h, r0:r0 + DN_CHUNK, :]
        k = k_ref[h, r0:r0 + DN_CHUNK, :]
        v = v_ref[h, r0:r0 + DN_CHUNK, :]
        gcl = gcol_ref[h, r0:r0 + DN_CHUNK, :]
        bt = bcol_ref[h, r0:r0 + DN_CHUNK, :]
        eg = jnp.exp(gcl)
        rhs = jnp.concatenate([v * bt, k * bt * eg], axis=1).astype(BF16)
        uw = _dot(t_c.astype(BF16), rhs)
        glast = gcl[DN_CHUNK - 1:DN_CHUNK, :]
        sb = st.astype(BF16)
        v_new = uw[:, :DN_DIM] - _dot(uw[:, DN_DIM:].astype(BF16), sb)
        vnb = v_new.astype(BF16)
        o = _dot((q * eg).astype(BF16), sb) + _dot(attn, vnb)
        st = st * jnp.exp(glast) + _dot_tn((k * jnp.exp(glast - gcl)).astype(BF16), vnb)
        zz = z_ref[h, r0:r0 + DN_CHUNK, :]
        o = o * lax.rsqrt(jnp.mean(o * o, axis=-1, keepdims=True) + EPS) * onorm_ref[...] * jax.nn.silu(zz)
        og_ref[h, r0:r0 + DN_CHUNK, :] = o
        return st

    def group_body(hg, carry):
        heads = [hg * DN_GROUP + g for g in range(DN_GROUP)]
        sts = [state_ref[h] for h in heads]
        for ch0 in range(0, n_ch, 2):
            preps = [(chunk_prep(h, ch0), chunk_prep(h, ch0 + 1)) for h in heads]
            nl2s = [jnp.concatenate([p[0][0], p[1][0]], axis=1) for p in preps]
            tms = [eye_pair + jnp.where((ri_pair >> 1) == (ci_pair >> 1), nl2, 0.0) for nl2 in nl2s]
            blk = 2
            while blk < DN_CHUNK:
                sel = (((ri_pair // (2 * blk)) == (ci_pair // (2 * blk)))
                       & ((ri_pair & (2 * blk - 1)) >= blk) & ((ci_pair & (2 * blk - 1)) < blk))
                tmbs = [tm.astype(BF16) for tm in tms]
                ys = [_dot(jnp.where(sel, nl2, 0.0).astype(BF16), block_diag(tmb))
                      for nl2, tmb in zip(nl2s, tmbs)]
                tms = [tm + _dot(tmb, block_diag(y.astype(BF16))) for tm, tmb, y in zip(tms, tmbs, ys)]
                blk *= 2
            for idx in range(2):
                sts = [chunk_state(h, ch0 + idx, tm[:, idx * DN_CHUNK:(idx + 1) * DN_CHUNK], p[idx][1], st)
                       for h, tm, p, st in zip(heads, tms, preps, sts)]
        for h, st in zip(heads, sts):
            state_ref[h] = st
        return carry

    if DN_HEADS == DN_GROUP:
        group_body(0, 0)
    else:
        lax.fori_loop(0, DN_HEADS // DN_GROUP, group_body, 0)

    for h in range(DN_HEADS):
        mix_ref[:, h * DN_DIM:(h + 1) * DN_DIM] = og_ref[h].astype(BF16)
    y = _dot(mix_ref[...], wout_ref[...])
    x_new = x_in + mod_ref[2:3, :] * y
    xo_ref[...] = x_new
    _ffn_prep(x_new, (b == 0) & (s == 0), mod_ref, nffn_ref, wr_ref, br_ref, cnt_ref, hn2_ref, ids_ref,
              gates_ref, counts_ref, ts)


def _odd_layer(x, prev, mod_l, nmix, nffn, w_in, conv_w, a_log, dt_bias, onorm, w_out, w_r, b_r):
    ts = TS_ODD
    w_main = w_in[:, :4 * DN_WIDTH].astype(BF16)
    w_ba = _split_hi_lo(jnp.pad(w_in[:, 4 * DN_WIDTH:], ((0, 0), (0, LANES - 2 * DN_HEADS))))
    lane_pad = (DN_HEADS, LANES - 2 * DN_HEADS)
    params = [nmix.reshape(1, -1), nffn.reshape(1, -1), w_main, w_ba,
              jnp.pad(conv_w, ((0, SUBLANES - DN_CONV_K), (0, 0))),
              jnp.pad(a_log, lane_pad).reshape(1, -1), jnp.pad(dt_bias, lane_pad).reshape(1, -1),
              onorm.reshape(1, -1), w_out.astype(BF16), w_r, b_r]
    head = (DN_HEADS, ts, DN_DIM)
    scratch = [pltpu.VMEM((DN_HALO + ts, 3 * DN_WIDTH), F32),
               pltpu.VMEM(head, F32), pltpu.VMEM(head, F32), pltpu.VMEM(head, F32), pltpu.VMEM(head, F32),
               pltpu.VMEM(head, F32), pltpu.VMEM(head, F32), pltpu.VMEM((DN_HEADS, SUBLANES, ts), F32),
               pltpu.VMEM(head, F32), pltpu.VMEM((DN_HEADS, DN_DIM, DN_DIM), F32),
               pltpu.VMEM((ts, D_MODEL), BF16), pltpu.VMEM((SUBLANES, LANES), F32)]
    return _mixer_call(_odd_kernel, ts, x, prev, mod_l, params, scratch, "deltanet_mixer")


def _expert_kernel(n_slots, blk_e_ref, tok_ref, tok_next_ref, slot_ref, hn2_ref, wup_ref, wdn_ref, yt_ref,
                   xg_ref, ys_ref, xb_ref, sem_in, sem_out):
    i = pl.program_id(0)
    n = pl.num_programs(0)
    cur = i % 2

    def gather_copy(tok, r, buf):
        src = hn2_ref.at[pl.ds(pl.multiple_of(tok * ROW_TILES, ROW_TILES), ROW_TILES), :]
        return pltpu.make_async_copy(src, xg_ref.at[buf, pl.ds(r * ROW_TILES, ROW_TILES), :], sem_in.at[buf])

    def scatter_copy(slot, r, buf):
        dst = yt_ref.at[pl.ds(pl.multiple_of(slot * ROW_TILES, ROW_TILES), ROW_TILES), :]
        return pltpu.make_async_copy(ys_ref.at[buf, pl.ds(r * ROW_TILES, ROW_TILES), :], dst, sem_out.at[buf])

    block_rows = pl.ds(0, MOE_BLOCK * ROW_TILES)

    def wait_gathers(buf):
        pltpu.make_async_copy(hn2_ref.at[block_rows, :], xg_ref.at[buf], sem_in.at[buf]).wait()

    def wait_scatters(buf):
        pltpu.make_async_copy(ys_ref.at[buf], yt_ref.at[block_rows, :], sem_out.at[buf]).wait()

    def prime_copy(buf):
        spare = pl.ds((n_slots + buf * MOE_BLOCK) * ROW_TILES, MOE_BLOCK * ROW_TILES)
        return pltpu.make_async_copy(ys_ref.at[buf], yt_ref.at[spare, :], sem_out.at[buf])

    @pl.when(i == 0)
    def _():
        ys_ref[...] = jnp.zeros_like(ys_ref)
        prime_copy(0).start()
        prime_copy(1).start()
        for r in range(MOE_BLOCK):
            gather_copy(tok_ref[0, r], r, 0).start()

    for r in range(MOE_BLOCK):
        gather_copy(tok_next_ref[0, r], r, 1 - cur).start()

    wait_gathers(cur)
    for j in range(ROW_TILES):
        xb_ref[:, j * LANES:(j + 1) * LANES] = xg_ref[cur, pl.ds(j, MOE_BLOCK, stride=ROW_TILES), :].astype(BF16)
    gu = _dot(xb_ref[...], wup_ref[...])
    hid = jax.nn.silu(gu[:, :D_EXPERT]) * gu[:, D_EXPERT:]
    y = _dot(hid.astype(BF16), wdn_ref[...])

    wait_scatters(cur)
    for j in range(ROW_TILES):
        ys_ref[cur, pl.ds(j, MOE_BLOCK, stride=ROW_TILES), :] = y[:, j * LANES:(j + 1) * LANES]
    for r in range(MOE_BLOCK):
        scatter_copy(slot_ref[0, r], r, cur).start()

    @pl.when(i == n - 1)
    def _():
        wait_scatters(cur)
        wait_scatters(1 - cur)
        wait_gathers(1 - cur)


def _expert_mlp(hn2, buf_tok, buf_slot, blk_e, n_slots, w_up, w_down):
    n_blk = blk_e.shape[0]
    idx_block = (None, 1, MOE_BLOCK)
    grid_spec = pltpu.PrefetchScalarGridSpec(
        num_scalar_prefetch=1,
        grid=(n_blk,),
        in_specs=[pl.BlockSpec(idx_block, lambda i, be: (i, 0, 0), memory_space=pltpu.SMEM),
                  pl.BlockSpec(idx_block, lambda i, be: (jnp.minimum(i + 1, n_blk - 1), 0, 0),
                               memory_space=pltpu.SMEM),
                  pl.BlockSpec(idx_block, lambda i, be: (i, 0, 0), memory_space=pltpu.SMEM),
                  pl.BlockSpec(memory_space=pl.ANY),
                  pl.BlockSpec((None, D_MODEL, 2 * D_EXPERT), lambda i, be: (be[i], 0, 0)),
                  pl.BlockSpec((None, D_EXPERT, D_MODEL), lambda i, be: (be[i], 0, 0))],
        out_specs=pl.BlockSpec(memory_space=pl.ANY),
        scratch_shapes=[pltpu.VMEM((2, MOE_BLOCK * ROW_TILES, LANES), F32),
                        pltpu.VMEM((2, MOE_BLOCK * ROW_TILES, LANES), F32),
                        pltpu.VMEM((MOE_BLOCK, D_MODEL), BF16),
                        pltpu.SemaphoreType.DMA((2,)), pltpu.SemaphoreType.DMA((2,))],
    )
    tok3 = buf_tok.reshape(n_blk, 1, MOE_BLOCK)
    return pl.pallas_call(
        functools.partial(_expert_kernel, n_slots),
        grid_spec=grid_spec,
        out_shape=jax.ShapeDtypeStruct(((n_slots + 2 * MOE_BLOCK) * ROW_TILES, LANES), F32),
        compiler_params=pltpu.CompilerParams(dimension_semantics=("arbitrary",), vmem_limit_bytes=VMEM_LIMIT),
        name="expert_mlp",
    )(blk_e, tok3, tok3, buf_slot.reshape(n_blk, 1, MOE_BLOCK), hn2, w_up, w_down)


def _moe(hn2, ids, counts, w_up, w_down):
    n_tok = ids.shape[0]
    m = n_tok * TOP_K
    p_rows = (m + N_EXPERTS * (MOE_BLOCK - 1) + MOE_BLOCK - 1) // MOE_BLOCK * MOE_BLOCK
    n_blk = p_rows // MOE_BLOCK
    eid = ids[:, 0:TOP_K]
    rank = ids[:, TOP_K:2 * TOP_K]
    cnt = counts[0, ROUTER_E0:ROUTER_E0 + N_EXPERTS].astype(jnp.int32)
    padded = (cnt + MOE_BLOCK - 1) // MOE_BLOCK * MOE_BLOCK
    pend = jnp.cumsum(padded)
    pstart = pend - padded
    dest = (pstart[eid] + rank).reshape(-1)
    slot_of_row = jnp.full((p_rows,), -1, jnp.int32).at[dest].set(jnp.arange(m, dtype=jnp.int32))
    is_pad = slot_of_row < 0
    pad_rank = jnp.cumsum(is_pad.astype(jnp.int32)) - 1
    buf_slot = jnp.where(is_pad, m + pad_rank, slot_of_row)
    buf_tok = jnp.where(is_pad, 0, slot_of_row // TOP_K)
    blk_start = jnp.arange(n_blk, dtype=jnp.int32) * MOE_BLOCK
    blk_e = jnp.minimum(jnp.sum(pend[None, :] <= blk_start[:, None], axis=1), N_EXPERTS - 1).astype(jnp.int32)
    return _expert_mlp(hn2, buf_tok, buf_slot, blk_e, p_rows, w_up, w_down)


def _final_kernel(ts, x_ref, yt_ref, pg_ref, pmod_ref, gain_ref, o_ref):
    x = _residual_in(x_ref, (yt_ref, pg_ref, pmod_ref), ts)
    ms = jnp.mean(x * x, axis=-1, keepdims=True)
    o_ref[...] = x * lax.rsqrt(ms + EPS) * gain_ref[...]


def _final(x, prev, gain):
    yt, pgates, pmod = prev
    ts = TS_EVEN
    n_tok = x.shape[0]
    bsz = pmod.shape[0]
    n_s = n_tok // bsz // ts
    tok_map = lambda b, s: (b * n_s + s, 0)
    return pl.pallas_call(
        functools.partial(_final_kernel, ts),
        grid=(bsz, n_s),
        in_specs=[pl.BlockSpec((ts, D_MODEL), tok_map),
                  pl.BlockSpec((ts * TOP_K * ROW_TILES, LANES), tok_map),
                  pl.BlockSpec((ts, LANES), tok_map),
                  pl.BlockSpec((None, SUBLANES, D_MODEL), lambda b, s: (b, 0, 0)),
                  pl.BlockSpec((1, D_MODEL), lambda b, s: (0, 0))],
        out_specs=pl.BlockSpec((ts, D_MODEL), tok_map),
        out_shape=jax.ShapeDtypeStruct((n_tok, D_MODEL), F32),
        compiler_params=pltpu.CompilerParams(dimension_semantics=("arbitrary", "arbitrary"),
                                             vmem_limit_bytes=VMEM_LIMIT),
        name="final_norm",
    )(x, yt, pgates, pmod, gain.reshape(1, -1))


def kernel(x, c, mod_w, mod_b, norm_mix, norm_ffn, ab_w_in, pool_w, pool_scale, conv_w, conv_b, conv_ln_g,
           conv_ln_b, ab_w_out, dn_w_in, dn_conv_w, dn_a_log, dn_dt_bias, dn_onorm, dn_w_out, moe_w_grp,
           moe_b_grp, moe_w_exp, moe_b_exp, moe_w_up, moe_w_down, final_norm):
    bsz, seq, d = x.shape
    depth = mod_w.shape[0]
    assert d == D_MODEL and seq % TS_EVEN == 0 and seq % TS_ODD == 0
    mod = _modulation(c, mod_w, mod_b)
    xt = x.reshape(bsz * seq, d)
    prev = None
    for l in range(depth):
        i = l // 2
        w_r = _split_hi_lo(jnp.pad(jnp.concatenate([moe_w_grp[l], moe_w_exp[l]], axis=1),
                                   ((0, 0), (0, LANES - N_GROUPS - N_EXPERTS))))
        b_r = jnp.pad(jnp.concatenate([moe_b_grp[l], moe_b_exp[l]]), (0, LANES - N_GROUPS - N_EXPERTS))
        b_r = b_r.reshape(1, -1)
        if l % 2 == 0:
            outs = _even_layer(xt, prev, mod[l], norm_mix[l], norm_ffn[l], ab_w_in[i], pool_w[i], pool_scale[i],
                               conv_w[i], conv_b[i], conv_ln_g[i], conv_ln_b[i], ab_w_out[i], w_r, b_r)
        else:
            outs = _odd_layer(xt, prev, mod[l], norm_mix[l], norm_ffn[l], dn_w_in[i], dn_conv_w[i], dn_a_log[i],
                              dn_dt_bias[i], dn_onorm[i], dn_w_out[i], w_r, b_r)
        xt, hn2, ids, gates, counts = outs
        yt = _moe(hn2, ids, counts, moe_w_up[l].astype(BF16), moe_w_down[l].astype(BF16))
        prev = (yt, gates, mod[l])
    out = _final(xt, prev, final_norm)
    return out.reshape(bsz, seq, d)
```

```python
import functools

import jax
import jax.numpy as jnp
from jax import lax
from jax.experimental import pallas as pl
from jax.experimental.pallas import tpu as pltpu

F32 = jnp.float32
BF16 = jnp.bfloat16
HI = lax.Precision.HIGHEST

D_MODEL = 1024
EPS = 1e-6
LANES = 128
SUBLANES = 8
ROW_TILES = D_MODEL // LANES
VMEM_LIMIT = 56 * 1024 * 1024

POOL_WINDOWS = (2, 4, 8, 16)
POOL_GROUP = 128
POOL_WIDTH = 512
CONV_WIDTH = 512
CONV_K = 31
AB_IN = POOL_WIDTH + 2 * CONV_WIDTH
HALO = 32

DN_HEADS = 8
DN_DIM = 128
DN_WIDTH = DN_HEADS * DN_DIM
DN_CONV_K = 4
DN_CHUNK = 128
DN_HALO = 8
DN_GROUP = 8

N_GROUPS = 4
EPG = 8
N_EXPERTS = N_GROUPS * EPG
TOP_K = 2
D_EXPERT = 256
MOE_BLOCK = 128
ROUTER_E0 = N_GROUPS

TS_EVEN = 512
TS_ODD = 256

NEG = -1e30


def _dot(a, b):
    return jnp.dot(a, b, preferred_element_type=F32)


def _dot_nt(a, b):
    return lax.dot_general(a, b, (((1,), (1,)), ((), ())), preferred_element_type=F32)


def _dot_tn(a, b):
    return lax.dot_general(a, b, (((0,), (0,)), ((), ())), preferred_element_type=F32)


def _split_hi_lo(w):
    hi = w.astype(BF16)
    return jnp.stack([hi, (w - hi.astype(F32)).astype(BF16)])


def _dot_split(x, w_ref):
    xh = x.astype(BF16)
    xl = (x - xh.astype(F32)).astype(BF16)
    return _dot(xh, w_ref[0]) + (_dot(xl, w_ref[0]) + _dot(xh, w_ref[1]))


def _rms_mod(x, gain, scale, shift):
    ms = jnp.mean(x * x, axis=-1, keepdims=True)
    return x * lax.rsqrt(ms + EPS) * gain * (1.0 + scale) + shift


def _load_rows(ref, n, first, stride):
    return jnp.concatenate([ref[pl.ds(first + j, n, stride=stride), :] for j in range(ROW_TILES)], axis=1)


def _store_rows(ref, val, n):
    for j in range(ROW_TILES):
        ref[pl.ds(j, n, stride=ROW_TILES), :] = val[:, j * LANES:(j + 1) * LANES]


def _mod_kernel(c_ref, w_ref, b_ref, o_ref):
    ca = jax.nn.silu(c_ref[...])
    o_ref[...] = jnp.dot(ca, w_ref[...], precision=HI, preferred_element_type=F32) + b_ref[...]


def _modulation(c, mod_w, mod_b):
    depth = mod_w.shape[0]
    bsz = c.shape[0]
    n_col = mod_w.shape[2] // D_MODEL
    out = pl.pallas_call(
        _mod_kernel,
        grid=(depth, n_col),
        in_specs=[
            pl.BlockSpec((bsz, D_MODEL), lambda l, j: (0, 0)),
            pl.BlockSpec((None, D_MODEL, D_MODEL), lambda l, j: (l, 0, j)),
            pl.BlockSpec((None, 1, D_MODEL), lambda l, j: (l, 0, j)),
        ],
        out_specs=pl.BlockSpec((None, bsz, D_MODEL), lambda l, j: (l, 0, j)),
        out_shape=jax.ShapeDtypeStruct((depth, bsz, n_col * D_MODEL), F32),
        compiler_params=pltpu.CompilerParams(dimension_semantics=("arbitrary", "arbitrary"),
                                             vmem_limit_bytes=VMEM_LIMIT),
        name="modulation",
    )(c, mod_w, mod_b.reshape(depth, 1, n_col * D_MODEL))
    mod = out.reshape(depth, bsz, n_col, D_MODEL)
    return jnp.pad(mod, ((0, 0), (0, 0), (0, SUBLANES - n_col), (0, 0)))


def _residual_in(x_ref, prev, ts):
    x = x_ref[...]
    if prev is None:
        return x
    yt0_ref, yt1_ref, pg_ref, pmod_ref = prev
    y0 = _load_rows(yt0_ref, ts, 0, ROW_TILES)
    y1 = _load_rows(yt1_ref, ts, 0, ROW_TILES)
    pg = pg_ref[...]
    return x + pmod_ref[5:6, :] * (y0 * pg[:, 0:1] + y1 * pg[:, 1:2])


def _ffn_prep(x_new, first_step, mod_ref, nffn_ref, wr_ref, br_ref, cnt_ref, hn2_ref, ids_ref, gates_ref,
              counts_ref, ts):
    hn2 = _rms_mod(x_new, nffn_ref[...], mod_ref[4:5, :], mod_ref[3:4, :])
    _store_rows(hn2_ref, hn2, ts)
    lg = _dot_split(hn2, wr_ref) + br_ref[...]
    lane = lax.broadcasted_iota(jnp.int32, (ts, LANES), 1)
    is_grp = lane < N_GROUPS
    gl = jnp.where(is_grp, lg, NEG)
    gmax = jnp.max(gl, axis=-1, keepdims=True)
    gsum = jnp.sum(jnp.where(is_grp, jnp.exp(gl - gmax), 0.0), axis=-1, keepdims=True)
    gp = 1.0 / gsum
    gi = jnp.min(jnp.where(gl == gmax, lane, LANES), axis=-1, keepdims=True)
    lo = ROUTER_E0 + gi * EPG
    in_grp = (lane >= lo) & (lane < lo + EPG)
    el = jnp.where(in_grp, lg, NEG)
    m1 = jnp.max(el, axis=-1, keepdims=True)
    i1 = jnp.min(jnp.where(el == m1, lane, LANES), axis=-1, keepdims=True)
    el2 = jnp.where(lane == i1, NEG, el)
    m2 = jnp.max(el2, axis=-1, keepdims=True)
    i2 = jnp.min(jnp.where(el2 == m2, lane, LANES), axis=-1, keepdims=True)
    e2 = jnp.exp(m2 - m1)
    w1 = gp / (1.0 + e2)
    w2 = gp * e2 / (1.0 + e2)

    @pl.when(first_step)
    def _():
        cnt_ref[...] = jnp.zeros_like(cnt_ref)

    onehot = ((lane == i1) | (lane == i2)).astype(F32)
    r = lax.broadcasted_iota(jnp.int32, (ts, ts), 0)
    c = lax.broadcasted_iota(jnp.int32, (ts, ts), 1)
    before = (r > c).astype(BF16)
    run = _dot(before, onehot.astype(BF16)) + cnt_ref[0:1, :]
    rank1 = jnp.sum(jnp.where(lane == i1, run, 0.0), axis=-1, keepdims=True)
    rank2 = jnp.sum(jnp.where(lane == i2, run, 0.0), axis=-1, keepdims=True)
    total = cnt_ref[0:1, :] + jnp.sum(onehot, axis=0, keepdims=True)
    cnt_ref[...] = jnp.broadcast_to(total, cnt_ref.shape)
    counts_ref[...] = jnp.broadcast_to(total, counts_ref.shape)

    ids = jnp.where(lane == 0, i1 - ROUTER_E0,
                    jnp.where(lane == 1, i2 - ROUTER_E0,
                              jnp.where(lane == 2, rank1.astype(jnp.int32),
                                        jnp.where(lane == 3, rank2.astype(jnp.int32), 0))))
    ids_ref[...] = ids
    gates_ref[...] = jnp.where(lane == 0, w1, jnp.where(lane == 1, w2, 0.0))


def _mixer_call(body, ts, x, prev, mod_l, params, scratch, name):
    n_tok = x.shape[0]
    bsz = mod_l.shape[0]
    seq = n_tok // bsz
    n_s = seq // ts
    tok_map = lambda b, s: (b * n_s + s, 0)
    in_specs = [pl.BlockSpec((ts, D_MODEL), tok_map)]
    args = [x]
    if prev is not None:
        yt, pgates, pmod = prev
        n_tiles = n_tok // ts
        in_specs += [pl.BlockSpec((ts * ROW_TILES, LANES), tok_map),
                     pl.BlockSpec((ts * ROW_TILES, LANES), lambda b, s: (n_tiles + b * n_s + s, 0)),
                     pl.BlockSpec((ts, LANES), tok_map),
                     pl.BlockSpec((None, SUBLANES, D_MODEL), lambda b, s: (b, 0, 0))]
        args += [yt, yt, pgates, pmod]
    in_specs.append(pl.BlockSpec((None, SUBLANES, D_MODEL), lambda b, s: (b, 0, 0)))
    args.append(mod_l)
    for arr in params:
        nd = arr.ndim
        in_specs.append(pl.BlockSpec(arr.shape, lambda b, s, _nd=nd: (0,) * _nd))
        args.append(arr)
    out_shape = [
        jax.ShapeDtypeStruct((n_tok, D_MODEL), F32),
        jax.ShapeDtypeStruct((n_tok * ROW_TILES, LANES), F32),
        jax.ShapeDtypeStruct((n_tok, LANES), jnp.int32),
        jax.ShapeDtypeStruct((n_tok, LANES), F32),
        jax.ShapeDtypeStruct((SUBLANES, LANES), F32),
    ]
    out_specs = [
        pl.BlockSpec((ts, D_MODEL), tok_map),
        pl.BlockSpec((ts * ROW_TILES, LANES), tok_map),
        pl.BlockSpec((ts, LANES), tok_map),
        pl.BlockSpec((ts, LANES), tok_map),
        pl.BlockSpec((SUBLANES, LANES), lambda b, s: (0, 0)),
    ]
    return pl.pallas_call(
        functools.partial(body, prev is not None, ts),
        grid=(bsz, n_s),
        in_specs=in_specs,
        out_specs=out_specs,
        out_shape=out_shape,
        scratch_shapes=scratch,
        compiler_params=pltpu.CompilerParams(dimension_semantics=("arbitrary", "arbitrary"),
                                             vmem_limit_bytes=VMEM_LIMIT),
        name=name,
    )(*args)


def _even_kernel(has_prev, ts, *refs):
    n_prev = 4 if has_prev else 0
    x_ref = refs[0]
    prev = refs[1:1 + n_prev] if has_prev else None
    (mod_ref, nmix_ref, nffn_ref, win_ref, poolw_ref, pscale_ref, convw_ref, convb_ref, lng_ref, lnb_ref,
     wout_ref, wr_ref, br_ref) = refs[1 + n_prev:14 + n_prev]
    xo_ref, hn2_ref, ids_ref, gates_ref, counts_ref = refs[14 + n_prev:19 + n_prev]
    extp_ref, extc_ref, shift_ref, mix_ref, cnt_ref = refs[19 + n_prev:]

    b = pl.program_id(0)
    s = pl.program_id(1)
    x_in = _residual_in(x_ref, prev, ts)
    hn = _rms_mod(x_in, nmix_ref[...], mod_ref[1:2, :], mod_ref[0:1, :])
    u = _dot(hn.astype(BF16), win_ref[...])

    @pl.when(s == 0)
    def _():
        extp_ref[0:HALO, :] = jnp.zeros((HALO, POOL_WIDTH), F32)
        extc_ref[0:HALO, :] = jnp.zeros((HALO, CONV_WIDTH), F32)

    @pl.when(s > 0)
    def _():
        extp_ref[0:HALO, :] = extp_ref[ts:ts + HALO, :]
        extc_ref[0:HALO, :] = extc_ref[ts:ts + HALO, :]

    extp_ref[HALO:HALO + ts, :] = u[:, :POOL_WIDTH]
    extc_ref[HALO:HALO + ts, :] = (u[:, POOL_WIDTH:POOL_WIDTH + CONV_WIDTH]
                                   * jax.nn.sigmoid(u[:, POOL_WIDTH + CONV_WIDTH:]))

    pos = (s * ts + 1 + lax.broadcasted_iota(jnp.int32, (ts, 1), 0)).astype(F32)
    for j, w in enumerate(POOL_WINDOWS):
        c0 = j * POOL_GROUP
        a = extp_ref[HALO:HALO + ts, c0:c0 + POOL_GROUP]
        win = a
        for d in range(1, w):
            win = win + extp_ref[HALO - d:HALO - d + ts, c0:c0 + POOL_GROUP]
        pooled = win / jnp.minimum(pos, float(w)) - a
        ya = _dot(pooled.astype(BF16), poolw_ref[j]) * pscale_ref[:, c0:c0 + POOL_GROUP]
        mix_ref[:, c0:c0 + POOL_GROUP] = ya.astype(BF16)

    for sh in range(1, SUBLANES):
        shift_ref[sh - 1, SUBLANES:HALO + ts, :] = extc_ref[SUBLANES - sh:HALO + ts - sh, :]
    rc = 32

    def conv_chunk(ci, carry):
        base = pl.multiple_of(ci * rc, rc)
        acc = jnp.broadcast_to(convb_ref[...], (rc, CONV_WIDTH))
        for k in range(CONV_K):
            delay = CONV_K - 1 - k
            a8, sh = delay // SUBLANES, delay % SUBLANES
            rows = pl.ds(base + (HALO - a8 * SUBLANES), rc)
            tap = extc_ref[rows, :] if sh == 0 else shift_ref[sh - 1, rows, :]
            acc = acc + convw_ref[k:k + 1, :] * tap
        mu = jnp.mean(acc, axis=-1, keepdims=True)
        cen = acc - mu
        var = jnp.mean(cen * cen, axis=-1, keepdims=True)
        yb = cen * lax.rsqrt(var + EPS) * lng_ref[...] + lnb_ref[...]
        mix_ref[pl.ds(base, rc), POOL_WIDTH:POOL_WIDTH + CONV_WIDTH] = jax.nn.silu(yb).astype(BF16)
        return carry

    lax.fori_loop(0, ts // rc, conv_chunk, 0)

    y = _dot(mix_ref[...], wout_ref[...])
    x_new = x_in + mod_ref[2:3, :] * y
    xo_ref[...] = x_new
    _ffn_prep(x_new, (b == 0) & (s == 0), mod_ref, nffn_ref, wr_ref, br_ref, cnt_ref, hn2_ref, ids_ref,
              gates_ref, counts_ref, ts)


def _even_layer(x, prev, mod_l, nmix, nffn, w_in, pool_w, pool_scale, conv_w, conv_b, ln_g, ln_b, w_out, w_r,
                b_r):
    ts = TS_EVEN
    params = [nmix.reshape(1, -1), nffn.reshape(1, -1), w_in.astype(BF16), pool_w.astype(BF16),
              pool_scale.reshape(1, -1), jnp.pad(conv_w, ((0, HALO - CONV_K), (0, 0))), conv_b.reshape(1, -1),
              ln_g.reshape(1, -1), ln_b.reshape(1, -1), w_out.astype(BF16), w_r, b_r]
    scratch = [pltpu.VMEM((HALO + ts, POOL_WIDTH), F32), pltpu.VMEM((HALO + ts, CONV_WIDTH), F32),
               pltpu.VMEM((SUBLANES - 1, HALO + ts, CONV_WIDTH), F32),
               pltpu.VMEM((ts, D_MODEL), BF16), pltpu.VMEM((SUBLANES, LANES), F32)]
    return _mixer_call(_even_kernel, ts, x, prev, mod_l, params, scratch, "even_mixer")


def _odd_kernel(has_prev, ts, *refs):
    n_prev = 4 if has_prev else 0
    x_ref = refs[0]
    prev = refs[1:1 + n_prev] if has_prev else None
    (mod_ref, nmix_ref, nffn_ref, win_ref, wba_ref, convw_ref, alog_ref, dtb_ref, onorm_ref, wout_ref, wr_ref,
     br_ref) = refs[1 + n_prev:13 + n_prev]
    xo_ref, hn2_ref, ids_ref, gates_ref, counts_ref = refs[13 + n_prev:18 + n_prev]
    (ext_ref, q_ref, k_ref, v_ref, z_ref, gcol_ref, bcol_ref, grow_ref, og_ref, state_ref, mix_ref,
     cnt_ref) = refs[18 + n_prev:]

    b = pl.program_id(0)
    s = pl.program_id(1)
    n_ch = ts // DN_CHUNK
    x_in = _residual_in(x_ref, prev, ts)
    hn = _rms_mod(x_in, nmix_ref[...], mod_ref[1:2, :], mod_ref[0:1, :])
    hb = hn.astype(BF16)

    @pl.when(s == 0)
    def _():
        ext_ref[0:DN_HALO, :] = jnp.zeros((DN_HALO, 3 * DN_WIDTH), F32)
        state_ref[...] = jnp.zeros_like(state_ref)

    @pl.when(s > 0)
    def _():
        ext_ref[0:DN_HALO, :] = ext_ref[ts:ts + DN_HALO, :]

    for g in range(3):
        ext_ref[DN_HALO:DN_HALO + ts, g * DN_WIDTH:(g + 1) * DN_WIDTH] = _dot(
            hb, win_ref[:, g * DN_WIDTH:(g + 1) * DN_WIDTH])
    z = _dot(hb, win_ref[:, 3 * DN_WIDTH:4 * DN_WIDTH])
    for h in range(DN_HEADS):
        z_ref[h] = z[:, h * DN_DIM:(h + 1) * DN_DIM]

    for g in range(3):
        for h in range(DN_HEADS):
            c0 = g * DN_WIDTH + h * DN_DIM
            acc = None
            for k in range(DN_CONV_K):
                off = DN_HALO - (DN_CONV_K - 1) + k
                term = convw_ref[k:k + 1, c0:c0 + DN_DIM] * ext_ref[off:off + ts, c0:c0 + DN_DIM]
                acc = term if acc is None else acc + term
            cv = jax.nn.silu(acc)
            if g == 0:
                q_ref[h] = cv * lax.rsqrt(jnp.sum(cv * cv, axis=-1, keepdims=True) + EPS) * (DN_DIM ** -0.5)
            elif g == 1:
                k_ref[h] = cv * lax.rsqrt(jnp.sum(cv * cv, axis=-1, keepdims=True) + EPS)
            else:
                v_ref[h] = cv

    ba = _dot_split(hn, wba_ref)
    beta = jax.nn.sigmoid(ba)
    gdec = -jnp.exp(alog_ref[...]) * jax.nn.softplus(ba + dtb_ref[...])
    r = lax.broadcasted_iota(jnp.int32, (ts, ts), 0)
    c = lax.broadcasted_iota(jnp.int32, (ts, ts), 1)
    same_chunk = (r // DN_CHUNK) == (c // DN_CHUNK)
    tri = ((r >= c) & same_chunk).astype(BF16)
    g_hi = gdec.astype(BF16)
    g_r1 = gdec - g_hi.astype(F32)
    g_mid = g_r1.astype(BF16)
    g_lo = (g_r1 - g_mid.astype(F32)).astype(BF16)
    gc = _dot(tri, g_hi) + (_dot(tri, g_mid) + _dot(tri, g_lo))
    gct = gc.T
    for h in range(DN_HEADS):
        gcol_ref[h] = jnp.broadcast_to(gc[:, DN_HEADS + h:DN_HEADS + h + 1], (ts, LANES))
        bcol_ref[h] = jnp.broadcast_to(beta[:, h:h + 1], (ts, LANES))
        grow_ref[h] = jnp.broadcast_to(gct[DN_HEADS + h:DN_HEADS + h + 1, :], (SUBLANES, ts))

    ri = lax.broadcasted_iota(jnp.int32, (DN_CHUNK, DN_CHUNK), 0)
    ci = lax.broadcasted_iota(jnp.int32, (DN_CHUNK, DN_CHUNK), 1)
    causal = ri >= ci
    strict = ri > ci
    eye = (ri == ci).astype(F32)

    ci_pair = lax.broadcasted_iota(jnp.int32, (DN_CHUNK, 2 * DN_CHUNK), 1) & (DN_CHUNK - 1)
    ri_pair = lax.broadcasted_iota(jnp.int32, (DN_CHUNK, 2 * DN_CHUNK), 0)
    eye_pair = (ri_pair == ci_pair).astype(F32)
    zero_blk = jnp.zeros((DN_CHUNK, DN_CHUNK), BF16)

    def block_diag(p):
        return jnp.concatenate([jnp.concatenate([p[:, :DN_CHUNK], zero_blk], axis=1),
                                jnp.concatenate([zero_blk, p[:, DN_CHUNK:]], axis=1)], axis=0)

    def chunk_prep(h, ch):
        r0 = ch * DN_CHUNK
        q = q_ref[h, r0:r0 + DN_CHUNK, :]
        k = k_ref[h, r0:r0 + DN_CHUNK, :]
        gcl = gcol_ref[h, r0:r0 + DN_CHUNK, :]
        grw = grow_ref[h, 0:1, r0:r0 + DN_CHUNK]
        bt = bcol_ref[h, r0:r0 + DN_CHUNK, :]
        decay = jnp.where(causal, jnp.exp(jnp.where(causal, gcl - grw, 0.0)), 0.0)
        kbf = k.astype(BF16)
        nl = jnp.where(strict, -(_dot_nt((k * bt).astype(BF16), kbf) * decay), 0.0)
        attn = jnp.where(causal, _dot_nt(q.astype(BF16), kbf) * decay, 0.0)
        return nl, attn.astype(BF16)

    def chunk_state(h, ch, t_c, attn, st):
        r0 = ch * DN_CHUNK
        q = q_ref[h, r0:r0 + DN_CHUNK, :]
        k = k_ref[h, r0:r0 + DN_CHUNK, :]
        v = v_ref[h, r0:r0 + DN_CHUNK, :]
        gcl = gcol_ref[h, r0:r0 + DN_CHUNK, :]
        bt = bcol_ref[h, r0:r0 + DN_CHUNK, :]
        eg = jnp.exp(gcl)
        rhs = jnp.concatenate([v * bt, k * bt * eg], axis=1).astype(BF16)
        uw = _dot(t_c.astype(BF16), rhs)
        glast = gcl[DN_CHUNK - 1:DN_CHUNK, :]
        sb = st.astype(BF16)
        v_new = uw[:, :DN_DIM] - _dot(uw[:, DN_DIM:].astype(BF16), sb)
        vnb = v_new.astype(BF16)
        o = _dot((q * eg).astype(BF16), sb) + _dot(attn, vnb)
        st = st * jnp.exp(glast) + _dot_tn((k * jnp.exp(glast - gcl)).astype(BF16), vnb)
        zz = z_ref[h, r0:r0 + DN_CHUNK, :]
        o = o * lax.rsqrt(jnp.mean(o * o, axis=-1, keepdims=True) + EPS) * onorm_ref[...] * jax.nn.silu(zz)
        og_ref[h, r0:r0 + DN_CHUNK, :] = o
        return st

    def group_body(hg, carry):
        heads = [hg * DN_GROUP + g for g in range(DN_GROUP)]
        sts = [state_ref[h] for h in heads]
        for ch0 in range(0, n_ch, 2):
            preps = [(chunk_prep(h, ch0), chunk_prep(h, ch0 + 1)) for h in heads]
            nl2s = [jnp.concatenate([p[0][0], p[1][0]], axis=1) for p in preps]
            tms = [eye_pair + jnp.where((ri_pair >> 1) == (ci_pair >> 1), nl2, 0.0) for nl2 in nl2s]
            blk = 2
            while blk < DN_CHUNK:
                sel = (((ri_pair // (2 * blk)) == (ci_pair // (2 * blk)))
                       & ((ri_pair & (2 * blk - 1)) >= blk) & ((ci_pair & (2 * blk - 1)) < blk))
                tmbs = [tm.astype(BF16) for tm in tms]
                ys = [_dot(jnp.where(sel, nl2, 0.0).astype(BF16), block_diag(tmb))
                      for nl2, tmb in zip(nl2s, tmbs)]
                tms = [tm + _dot(tmb, block_diag(y.astype(BF16))) for tm, tmb, y in zip(tms, tmbs, ys)]
                blk *= 2
            for idx in range(2):
                sts = [chunk_state(h, ch0 + idx, tm[:, idx * DN_CHUNK:(idx + 1) * DN_CHUNK], p[idx][1], st)
                       for h, tm, p, st in zip(heads, tms, preps, sts)]
        for h, st in zip(heads, sts):
            state_ref[h] = st
        return carry

    if DN_HEADS == DN_GROUP:
        group_body(0, 0)
    else:
        lax.fori_loop(0, DN_HEADS // DN_GROUP, group_body, 0)

    for h in range(DN_HEADS):
        mix_ref[:, h * DN_DIM:(h + 1) * DN_DIM] = og_ref[h].astype(BF16)
    y = _dot(mix_ref[...], wout_ref[...])
    x_new = x_in + mod_ref[2:3, :] * y
    xo_ref[...] = x_new
    _ffn_prep(x_new, (b == 0) & (s == 0), mod_ref, nffn_ref, wr_ref, br_ref, cnt_ref, hn2_ref, ids_ref,
              gates_ref, counts_ref, ts)


def _odd_layer(x, prev, mod_l, nmix, nffn, w_in, conv_w, a_log, dt_bias, onorm, w_out, w_r, b_r):
    ts = TS_ODD
    w_main = w_in[:, :4 * DN_WIDTH].astype(BF16)
    w_ba = _split_hi_lo(jnp.pad(w_in[:, 4 * DN_WIDTH:], ((0, 0), (0, LANES - 2 * DN_HEADS))))
    lane_pad = (DN_HEADS, LANES - 2 * DN_HEADS)
    params = [nmix.reshape(1, -1), nffn.reshape(1, -1), w_main, w_ba,
              jnp.pad(conv_w, ((0, SUBLANES - DN_CONV_K), (0, 0))),
              jnp.pad(a_log, lane_pad).reshape(1, -1), jnp.pad(dt_bias, lane_pad).reshape(1, -1),
              onorm.reshape(1, -1), w_out.astype(BF16), w_r, b_r]
    head = (DN_HEADS, ts, DN_DIM)
    scratch = [pltpu.VMEM((DN_HALO + ts, 3 * DN_WIDTH), F32),
               pltpu.VMEM(head, F32), pltpu.VMEM(head, F32), pltpu.VMEM(head, F32), pltpu.VMEM(head, F32),
               pltpu.VMEM(head, F32), pltpu.VMEM(head, F32), pltpu.VMEM((DN_HEADS, SUBLANES, ts), F32),
               pltpu.VMEM(head, F32), pltpu.VMEM((DN_HEADS, DN_DIM, DN_DIM), F32),
               pltpu.VMEM((ts, D_MODEL), BF16), pltpu.VMEM((SUBLANES, LANES), F32)]
    return _mixer_call(_odd_kernel, ts, x, prev, mod_l, params, scratch, "deltanet_mixer")


def _expert_kernel(n_slots, blk_e_ref, tok_ref, tok_next_ref, slot_ref, hn2_ref, wup_ref, wdn_ref, yt_ref,
                   xg_ref, ys_ref, xb_ref, sem_in, sem_out):
    i = pl.program_id(0)
    n = pl.num_programs(0)
    cur = i % 2

    def gather_copy(tok, r, buf):
        src = hn2_ref.at[pl.ds(pl.multiple_of(tok * ROW_TILES, ROW_TILES), ROW_TILES), :]
        return pltpu.make_async_copy(src, xg_ref.at[buf, pl.ds(r * ROW_TILES, ROW_TILES), :], sem_in.at[buf])

    def scatter_copy(slot, r, buf):
        dst = yt_ref.at[pl.ds(pl.multiple_of(slot * ROW_TILES, ROW_TILES), ROW_TILES), :]
        return pltpu.make_async_copy(ys_ref.at[buf, pl.ds(r * ROW_TILES, ROW_TILES), :], dst, sem_out.at[buf])

    block_rows = pl.ds(0, MOE_BLOCK * ROW_TILES)

    def wait_gathers(buf):
        pltpu.make_async_copy(hn2_ref.at[block_rows, :], xg_ref.at[buf], sem_in.at[buf]).wait()

    def wait_scatters(buf):
        pltpu.make_async_copy(ys_ref.at[buf], yt_ref.at[block_rows, :], sem_out.at[buf]).wait()

    def prime_copy(buf):
        spare = pl.ds((n_slots + buf * MOE_BLOCK) * ROW_TILES, MOE_BLOCK * ROW_TILES)
        return pltpu.make_async_copy(ys_ref.at[buf], yt_ref.at[spare, :], sem_out.at[buf])

    @pl.when(i == 0)
    def _():
        ys_ref[...] = jnp.zeros_like(ys_ref)
        prime_copy(0).start()
        prime_copy(1).start()
        for r in range(MOE_BLOCK):
            gather_copy(tok_ref[0, r], r, 0).start()

    for r in range(MOE_BLOCK):
        gather_copy(tok_next_ref[0, r], r, 1 - cur).start(priority=r % 2)

    wait_gathers(cur)
    for j in range(ROW_TILES):
        xb_ref[:, j * LANES:(j + 1) * LANES] = xg_ref[cur, pl.ds(j, MOE_BLOCK, stride=ROW_TILES), :].astype(BF16)
    gu = _dot(xb_ref[...], wup_ref[...])
    hid = jax.nn.silu(gu[:, :D_EXPERT]) * gu[:, D_EXPERT:]
    y = _dot(hid.astype(BF16), wdn_ref[...])

    wait_scatters(cur)
    for j in range(ROW_TILES):
        ys_ref[cur, pl.ds(j, MOE_BLOCK, stride=ROW_TILES), :] = y[:, j * LANES:(j + 1) * LANES]
    for r in range(MOE_BLOCK):
        scatter_copy(slot_ref[0, r], r, cur).start(priority=r % 2)

    @pl.when(i == n - 1)
    def _():
        wait_scatters(cur)
        wait_scatters(1 - cur)
        wait_gathers(1 - cur)


def _expert_mlp(hn2, buf_tok, buf_slot, blk_e, n_slots, w_up, w_down):
    n_blk = blk_e.shape[0]
    idx_block = (None, 1, MOE_BLOCK)
    grid_spec = pltpu.PrefetchScalarGridSpec(
        num_scalar_prefetch=1,
        grid=(n_blk,),
        in_specs=[pl.BlockSpec(idx_block, lambda i, be: (i, 0, 0), memory_space=pltpu.SMEM),
                  pl.BlockSpec(idx_block, lambda i, be: (jnp.minimum(i + 1, n_blk - 1), 0, 0),
                               memory_space=pltpu.SMEM),
                  pl.BlockSpec(idx_block, lambda i, be: (i, 0, 0), memory_space=pltpu.SMEM),
                  pl.BlockSpec(memory_space=pl.ANY),
                  pl.BlockSpec((None, D_MODEL, 2 * D_EXPERT), lambda i, be: (be[i], 0, 0)),
                  pl.BlockSpec((None, D_EXPERT, D_MODEL), lambda i, be: (be[i], 0, 0))],
        out_specs=pl.BlockSpec(memory_space=pl.ANY),
        scratch_shapes=[pltpu.VMEM((2, MOE_BLOCK * ROW_TILES, LANES), F32),
                        pltpu.VMEM((2, MOE_BLOCK * ROW_TILES, LANES), F32),
                        pltpu.VMEM((MOE_BLOCK, D_MODEL), BF16),
                        pltpu.SemaphoreType.DMA((2,)), pltpu.SemaphoreType.DMA((2,))],
    )
    tok3 = buf_tok.reshape(n_blk, 1, MOE_BLOCK)
    return pl.pallas_call(
        functools.partial(_expert_kernel, n_slots),
        grid_spec=grid_spec,
        out_shape=jax.ShapeDtypeStruct(((n_slots + 2 * MOE_BLOCK) * ROW_TILES, LANES), F32),
        compiler_params=pltpu.CompilerParams(dimension_semantics=("arbitrary",), vmem_limit_bytes=VMEM_LIMIT),
        name="expert_mlp",
    )(blk_e, tok3, tok3, buf_slot.reshape(n_blk, 1, MOE_BLOCK), hn2, w_up, w_down)


def _moe(hn2, ids, counts, w_up, w_down):
    n_tok = ids.shape[0]
    m = n_tok * TOP_K
    p_rows = (m + N_EXPERTS * (MOE_BLOCK - 1) + MOE_BLOCK - 1) // MOE_BLOCK * MOE_BLOCK
    n_blk = p_rows // MOE_BLOCK
    eid = ids[:, 0:TOP_K]
    rank = ids[:, TOP_K:2 * TOP_K]
    cnt = counts[0, ROUTER_E0:ROUTER_E0 + N_EXPERTS].astype(jnp.int32)
    padded = (cnt + MOE_BLOCK - 1) // MOE_BLOCK * MOE_BLOCK
    pend = jnp.cumsum(padded)
    pstart = pend - padded
    dest = (pstart[eid] + rank).reshape(-1)
    pair_of_row = jnp.full((p_rows,), -1, jnp.int32).at[dest].set(jnp.arange(m, dtype=jnp.int32))
    is_pad = pair_of_row < 0
    pad_rank = jnp.cumsum(is_pad.astype(jnp.int32)) - 1
    tok_of_row = pair_of_row // TOP_K
    buf_slot = jnp.where(is_pad, m + pad_rank, (pair_of_row % TOP_K) * n_tok + tok_of_row)
    buf_tok = jnp.where(is_pad, 0, tok_of_row)
    blk_start = jnp.arange(n_blk, dtype=jnp.int32) * MOE_BLOCK
    blk_e = jnp.minimum(jnp.sum(pend[None, :] <= blk_start[:, None], axis=1), N_EXPERTS - 1).astype(jnp.int32)
    return _expert_mlp(hn2, buf_tok, buf_slot, blk_e, p_rows, w_up, w_down)


def _final_kernel(ts, x_ref, yt0_ref, yt1_ref, pg_ref, pmod_ref, gain_ref, o_ref):
    x = _residual_in(x_ref, (yt0_ref, yt1_ref, pg_ref, pmod_ref), ts)
    ms = jnp.mean(x * x, axis=-1, keepdims=True)
    o_ref[...] = x * lax.rsqrt(ms + EPS) * gain_ref[...]


def _final(x, prev, gain):
    yt, pgates, pmod = prev
    ts = TS_EVEN
    n_tok = x.shape[0]
    bsz = pmod.shape[0]
    n_s = n_tok // bsz // ts
    tok_map = lambda b, s: (b * n_s + s, 0)
    return pl.pallas_call(
        functools.partial(_final_kernel, ts),
        grid=(bsz, n_s),
        in_specs=[pl.BlockSpec((ts, D_MODEL), tok_map),
                  pl.BlockSpec((ts * ROW_TILES, LANES), tok_map),
                  pl.BlockSpec((ts * ROW_TILES, LANES), lambda b, s: (n_tok // ts + b * n_s + s, 0)),
                  pl.BlockSpec((ts, LANES), tok_map),
                  pl.BlockSpec((None, SUBLANES, D_MODEL), lambda b, s: (b, 0, 0)),
                  pl.BlockSpec((1, D_MODEL), lambda b, s: (0, 0))],
        out_specs=pl.BlockSpec((ts, D_MODEL), tok_map),
        out_shape=jax.ShapeDtypeStruct((n_tok, D_MODEL), F32),
        compiler_params=pltpu.CompilerParams(dimension_semantics=("arbitrary", "arbitrary"),
                                             vmem_limit_bytes=VMEM_LIMIT),
        name="final_norm",
    )(x, yt, yt, pgates, pmod, gain.reshape(1, -1))


def kernel(x, c, mod_w, mod_b, norm_mix, norm_ffn, ab_w_in, pool_w, pool_scale, conv_w, conv_b, conv_ln_g,
           conv_ln_b, ab_w_out, dn_w_in, dn_conv_w, dn_a_log, dn_dt_bias, dn_onorm, dn_w_out, moe_w_grp,
           moe_b_grp, moe_w_exp, moe_b_exp, moe_w_up, moe_w_down, final_norm):
    bsz, seq, d = x.shape
    depth = mod_w.shape[0]
    assert d == D_MODEL and seq % TS_EVEN == 0 and seq % TS_ODD == 0
    mod = _modulation(c, mod_w, mod_b)
    xt = x.reshape(bsz * seq, d)
    prev = None
    for l in range(depth):
        i = l // 2
        w_r = _split_hi_lo(jnp.pad(jnp.concatenate([moe_w_grp[l], moe_w_exp[l]], axis=1),
                                   ((0, 0), (0, LANES - N_GROUPS - N_EXPERTS))))
        b_r = jnp.pad(jnp.concatenate([moe_b_grp[l], moe_b_exp[l]]), (0, LANES - N_GROUPS - N_EXPERTS))
        b_r = b_r.reshape(1, -1)
        if l % 2 == 0:
            outs = _even_layer(xt, prev, mod[l], norm_mix[l], norm_ffn[l], ab_w_in[i], pool_w[i], pool_scale[i],
                               conv_w[i], conv_b[i], conv_ln_g[i], conv_ln_b[i], ab_w_out[i], w_r, b_r)
        else:
            outs = _odd_layer(xt, prev, mod[l], norm_mix[l], norm_ffn[l], dn_w_in[i], dn_conv_w[i], dn_a_log[i],
                              dn_dt_bias[i], dn_onorm[i], dn_w_out[i], w_r, b_r)
        xt, hn2, ids, gates, counts = outs
        yt = _moe(hn2, ids, counts, moe_w_up[l].astype(BF16), moe_w_down[l].astype(BF16))
        prev = (yt, gates, mod[l])
    out = _final(xt, prev, final_norm)
    return out.reshape(bsz, seq, d)
```

```python
import functools

import jax
import jax.numpy as jnp
from jax import lax
from jax.experimental import pallas as pl
from jax.experimental.pallas import tpu as pltpu

F32 = jnp.float32
BF16 = jnp.bfloat16
HI = lax.Precision.HIGHEST

D_MODEL = 1024
EPS = 1e-6
LANES = 128
SUBLANES = 8
ROW_TILES = D_MODEL // LANES
VMEM_LIMIT = 56 * 1024 * 1024

POOL_WINDOWS = (2, 4, 8, 16)
POOL_GROUP = 128
POOL_WIDTH = 512
CONV_WIDTH = 512
CONV_K = 31
AB_IN = POOL_WIDTH + 2 * CONV_WIDTH
HALO = 32

DN_HEADS = 8
DN_DIM = 128
DN_WIDTH = DN_HEADS * DN_DIM
DN_CONV_K = 4
DN_CHUNK = 128
DN_HALO = 8
DN_GROUP = 8

N_GROUPS = 4
EPG = 8
N_EXPERTS = N_GROUPS * EPG
TOP_K = 2
D_EXPERT = 256
MOE_BLOCK = 256
ROUTER_E0 = N_GROUPS

TS_EVEN = 512
TS_ODD = 256

NEG = -1e30


def _dot(a, b):
    return jnp.dot(a, b, preferred_element_type=F32)


def _dot_nt(a, b):
    return lax.dot_general(a, b, (((1,), (1,)), ((), ())), preferred_element_type=F32)


def _dot_tn(a, b):
    return lax.dot_general(a, b, (((0,), (0,)), ((), ())), preferred_element_type=F32)


def _split_hi_lo(w):
    hi = w.astype(BF16)
    return jnp.stack([hi, (w - hi.astype(F32)).astype(BF16)])


def _dot_split(x, w_ref):
    xh = x.astype(BF16)
    xl = (x - xh.astype(F32)).astype(BF16)
    return _dot(xh, w_ref[0]) + (_dot(xl, w_ref[0]) + _dot(xh, w_ref[1]))


def _rms_mod(x, gain, scale, shift):
    ms = jnp.mean(x * x, axis=-1, keepdims=True)
    return x * lax.rsqrt(ms + EPS) * gain * (1.0 + scale) + shift


def _load_rows(ref, n, first, stride):
    return jnp.concatenate([ref[pl.ds(first + j, n, stride=stride), :] for j in range(ROW_TILES)], axis=1)


def _store_rows(ref, val, n):
    for j in range(ROW_TILES):
        ref[pl.ds(j, n, stride=ROW_TILES), :] = val[:, j * LANES:(j + 1) * LANES]


def _mod_kernel(c_ref, w_ref, b_ref, o_ref):
    ca = jax.nn.silu(c_ref[...])
    o_ref[...] = jnp.dot(ca, w_ref[...], precision=HI, preferred_element_type=F32) + b_ref[...]


def _modulation(c, mod_w, mod_b):
    depth = mod_w.shape[0]
    bsz = c.shape[0]
    n_col = mod_w.shape[2] // D_MODEL
    out = pl.pallas_call(
        _mod_kernel,
        grid=(depth, n_col),
        in_specs=[
            pl.BlockSpec((bsz, D_MODEL), lambda l, j: (0, 0)),
            pl.BlockSpec((None, D_MODEL, D_MODEL), lambda l, j: (l, 0, j)),
            pl.BlockSpec((None, 1, D_MODEL), lambda l, j: (l, 0, j)),
        ],
        out_specs=pl.BlockSpec((None, bsz, D_MODEL), lambda l, j: (l, 0, j)),
        out_shape=jax.ShapeDtypeStruct((depth, bsz, n_col * D_MODEL), F32),
        compiler_params=pltpu.CompilerParams(dimension_semantics=("arbitrary", "arbitrary"),
                                             vmem_limit_bytes=VMEM_LIMIT),
        name="modulation",
    )(c, mod_w, mod_b.reshape(depth, 1, n_col * D_MODEL))
    mod = out.reshape(depth, bsz, n_col, D_MODEL)
    return jnp.pad(mod, ((0, 0), (0, 0), (0, SUBLANES - n_col), (0, 0)))


def _residual_in(x_ref, prev, ts):
    x = x_ref[...]
    if prev is None:
        return x
    yt0_ref, yt1_ref, pg_ref, pmod_ref = prev
    y0 = _load_rows(yt0_ref, ts, 0, ROW_TILES)
    y1 = _load_rows(yt1_ref, ts, 0, ROW_TILES)
    pg = pg_ref[...]
    return x + pmod_ref[5:6, :] * (y0 * pg[:, 0:1] + y1 * pg[:, 1:2])


def _ffn_prep(x_new, first_step, mod_ref, nffn_ref, wr_ref, br_ref, cnt_ref, hn2_ref, ids_ref, gates_ref,
              counts_ref, ts):
    hn2 = _rms_mod(x_new, nffn_ref[...], mod_ref[4:5, :], mod_ref[3:4, :])
    _store_rows(hn2_ref, hn2, ts)
    lg = _dot_split(hn2, wr_ref) + br_ref[...]
    lane = lax.broadcasted_iota(jnp.int32, (ts, LANES), 1)
    is_grp = lane < N_GROUPS
    gl = jnp.where(is_grp, lg, NEG)
    gmax = jnp.max(gl, axis=-1, keepdims=True)
    gsum = jnp.sum(jnp.where(is_grp, jnp.exp(gl - gmax), 0.0), axis=-1, keepdims=True)
    gp = 1.0 / gsum
    gi = jnp.min(jnp.where(gl == gmax, lane, LANES), axis=-1, keepdims=True)
    lo = ROUTER_E0 + gi * EPG
    in_grp = (lane >= lo) & (lane < lo + EPG)
    el = jnp.where(in_grp, lg, NEG)
    m1 = jnp.max(el, axis=-1, keepdims=True)
    i1 = jnp.min(jnp.where(el == m1, lane, LANES), axis=-1, keepdims=True)
    el2 = jnp.where(lane == i1, NEG, el)
    m2 = jnp.max(el2, axis=-1, keepdims=True)
    i2 = jnp.min(jnp.where(el2 == m2, lane, LANES), axis=-1, keepdims=True)
    e2 = jnp.exp(m2 - m1)
    w1 = gp / (1.0 + e2)
    w2 = gp * e2 / (1.0 + e2)

    @pl.when(first_step)
    def _():
        cnt_ref[...] = jnp.zeros_like(cnt_ref)

    onehot = ((lane == i1) | (lane == i2)).astype(F32)
    r = lax.broadcasted_iota(jnp.int32, (ts, ts), 0)
    c = lax.broadcasted_iota(jnp.int32, (ts, ts), 1)
    before = (r > c).astype(BF16)
    run = _dot(before, onehot.astype(BF16)) + cnt_ref[0:1, :]
    rank1 = jnp.sum(jnp.where(lane == i1, run, 0.0), axis=-1, keepdims=True)
    rank2 = jnp.sum(jnp.where(lane == i2, run, 0.0), axis=-1, keepdims=True)
    total = cnt_ref[0:1, :] + jnp.sum(onehot, axis=0, keepdims=True)
    cnt_ref[...] = jnp.broadcast_to(total, cnt_ref.shape)
    counts_ref[...] = jnp.broadcast_to(total, counts_ref.shape)

    ids = jnp.where(lane == 0, i1 - ROUTER_E0,
                    jnp.where(lane == 1, i2 - ROUTER_E0,
                              jnp.where(lane == 2, rank1.astype(jnp.int32),
                                        jnp.where(lane == 3, rank2.astype(jnp.int32), 0))))
    ids_ref[...] = ids
    gates_ref[...] = jnp.where(lane == 0, w1, jnp.where(lane == 1, w2, 0.0))


def _mixer_call(body, ts, x, prev, mod_l, params, scratch, name):
    n_tok = x.shape[0]
    bsz = mod_l.shape[0]
    seq = n_tok // bsz
    n_s = seq // ts
    tok_map = lambda b, s: (b * n_s + s, 0)
    in_specs = [pl.BlockSpec((ts, D_MODEL), tok_map)]
    args = [x]
    if prev is not None:
        yt, pgates, pmod = prev
        n_tiles = n_tok // ts
        in_specs += [pl.BlockSpec((ts * ROW_TILES, LANES), tok_map),
                     pl.BlockSpec((ts * ROW_TILES, LANES), lambda b, s: (n_tiles + b * n_s + s, 0)),
                     pl.BlockSpec((ts, LANES), tok_map),
                     pl.BlockSpec((None, SUBLANES, D_MODEL), lambda b, s: (b, 0, 0))]
        args += [yt, yt, pgates, pmod]
    in_specs.append(pl.BlockSpec((None, SUBLANES, D_MODEL), lambda b, s: (b, 0, 0)))
    args.append(mod_l)
    for arr in params:
        nd = arr.ndim
        in_specs.append(pl.BlockSpec(arr.shape, lambda b, s, _nd=nd: (0,) * _nd))
        args.append(arr)
    out_shape = [
        jax.ShapeDtypeStruct((n_tok, D_MODEL), F32),
        jax.ShapeDtypeStruct((n_tok * ROW_TILES, LANES), F32),
        jax.ShapeDtypeStruct((n_tok, LANES), jnp.int32),
        jax.ShapeDtypeStruct((n_tok, LANES), F32),
        jax.ShapeDtypeStruct((SUBLANES, LANES), F32),
    ]
    out_specs = [
        pl.BlockSpec((ts, D_MODEL), tok_map),
        pl.BlockSpec((ts * ROW_TILES, LANES), tok_map),
        pl.BlockSpec((ts, LANES), tok_map),
        pl.BlockSpec((ts, LANES), tok_map),
        pl.BlockSpec((SUBLANES, LANES), lambda b, s: (0, 0)),
    ]
    return pl.pallas_call(
        functools.partial(body, prev is not None, ts),
        grid=(bsz, n_s),
        in_specs=in_specs,
        out_specs=out_specs,
        out_shape=out_shape,
        scratch_shapes=scratch,
        compiler_params=pltpu.CompilerParams(dimension_semantics=("arbitrary", "arbitrary"),
                                             vmem_limit_bytes=VMEM_LIMIT),
        name=name,
    )(*args)


def _even_kernel(has_prev, ts, *refs):
    n_prev = 4 if has_prev else 0
    x_ref = refs[0]
    prev = refs[1:1 + n_prev] if has_prev else None
    (mod_ref, nmix_ref, nffn_ref, win_ref, poolw_ref, pscale_ref, convw_ref, convb_ref, lng_ref, lnb_ref,
     wout_ref, wr_ref, br_ref) = refs[1 + n_prev:14 + n_prev]
    xo_ref, hn2_ref, ids_ref, gates_ref, counts_ref = refs[14 + n_prev:19 + n_prev]
    extp_ref, extc_ref, shift_ref, mix_ref, cnt_ref = refs[19 + n_prev:]

    b = pl.program_id(0)
    s = pl.program_id(1)
    x_in = _residual_in(x_ref, prev, ts)
    hn = _rms_mod(x_in, nmix_ref[...], mod_ref[1:2, :], mod_ref[0:1, :])
    u = _dot(hn.astype(BF16), win_ref[...])

    @pl.when(s == 0)
    def _():
        extp_ref[0:HALO, :] = jnp.zeros((HALO, POOL_WIDTH), F32)
        extc_ref[0:HALO, :] = jnp.zeros((HALO, CONV_WIDTH), F32)

    @pl.when(s > 0)
    def _():
        extp_ref[0:HALO, :] = extp_ref[ts:ts + HALO, :]
        extc_ref[0:HALO, :] = extc_ref[ts:ts + HALO, :]

    extp_ref[HALO:HALO + ts, :] = u[:, :POOL_WIDTH]
    extc_ref[HALO:HALO + ts, :] = (u[:, POOL_WIDTH:POOL_WIDTH + CONV_WIDTH]
                                   * jax.nn.sigmoid(u[:, POOL_WIDTH + CONV_WIDTH:]))

    pos = (s * ts + 1 + lax.broadcasted_iota(jnp.int32, (ts, 1), 0)).astype(F32)
    for j, w in enumerate(POOL_WINDOWS):
        c0 = j * POOL_GROUP
        a = extp_ref[HALO:HALO + ts, c0:c0 + POOL_GROUP]
        win = a
        for d in range(1, w):
            win = win + extp_ref[HALO - d:HALO - d + ts, c0:c0 + POOL_GROUP]
        pooled = win / jnp.minimum(pos, float(w)) - a
        ya = _dot(pooled.astype(BF16), poolw_ref[j]) * pscale_ref[:, c0:c0 + POOL_GROUP]
        mix_ref[:, c0:c0 + POOL_GROUP] = ya.astype(BF16)

    for sh in range(1, SUBLANES):
        shift_ref[sh - 1, SUBLANES:HALO + ts, :] = extc_ref[SUBLANES - sh:HALO + ts - sh, :]
    rc = 32

    def conv_chunk(ci, carry):
        base = pl.multiple_of(ci * rc, rc)
        acc = jnp.broadcast_to(convb_ref[...], (rc, CONV_WIDTH))
        for k in range(CONV_K):
            delay = CONV_K - 1 - k
            a8, sh = delay // SUBLANES, delay % SUBLANES
            rows = pl.ds(base + (HALO - a8 * SUBLANES), rc)
            tap = extc_ref[rows, :] if sh == 0 else shift_ref[sh - 1, rows, :]
            acc = acc + convw_ref[k:k + 1, :] * tap
        mu = jnp.mean(acc, axis=-1, keepdims=True)
        cen = acc - mu
        var = jnp.mean(cen * cen, axis=-1, keepdims=True)
        yb = cen * lax.rsqrt(var + EPS) * lng_ref[...] + lnb_ref[...]
        mix_ref[pl.ds(base, rc), POOL_WIDTH:POOL_WIDTH + CONV_WIDTH] = jax.nn.silu(yb).astype(BF16)
        return carry

    lax.fori_loop(0, ts // rc, conv_chunk, 0)

    y = _dot(mix_ref[...], wout_ref[...])
    x_new = x_in + mod_ref[2:3, :] * y
    xo_ref[...] = x_new
    _ffn_prep(x_new, (b == 0) & (s == 0), mod_ref, nffn_ref, wr_ref, br_ref, cnt_ref, hn2_ref, ids_ref,
              gates_ref, counts_ref, ts)


def _even_layer(x, prev, mod_l, nmix, nffn, w_in, pool_w, pool_scale, conv_w, conv_b, ln_g, ln_b, w_out, w_r,
                b_r):
    ts = TS_EVEN
    params = [nmix.reshape(1, -1), nffn.reshape(1, -1), w_in.astype(BF16), pool_w.astype(BF16),
              pool_scale.reshape(1, -1), jnp.pad(conv_w, ((0, HALO - CONV_K), (0, 0))), conv_b.reshape(1, -1),
              ln_g.reshape(1, -1), ln_b.reshape(1, -1), w_out.astype(BF16), w_r, b_r]
    scratch = [pltpu.VMEM((HALO + ts, POOL_WIDTH), F32), pltpu.VMEM((HALO + ts, CONV_WIDTH), F32),
               pltpu.VMEM((SUBLANES - 1, HALO + ts, CONV_WIDTH), F32),
               pltpu.VMEM((ts, D_MODEL), BF16), pltpu.VMEM((SUBLANES, LANES), F32)]
    return _mixer_call(_even_kernel, ts, x, prev, mod_l, params, scratch, "even_mixer")


def _odd_kernel(has_prev, ts, *refs):
    n_prev = 4 if has_prev else 0
    x_ref = refs[0]
    prev = refs[1:1 + n_prev] if has_prev else None
    (mod_ref, nmix_ref, nffn_ref, win_ref, wba_ref, convw_ref, alog_ref, dtb_ref, onorm_ref, wout_ref, wr_ref,
     br_ref) = refs[1 + n_prev:13 + n_prev]
    xo_ref, hn2_ref, ids_ref, gates_ref, counts_ref = refs[13 + n_prev:18 + n_prev]
    (ext_ref, q_ref, k_ref, v_ref, z_ref, gcol_ref, bcol_ref, grow_ref, og_ref, state_ref, mix_ref,
     cnt_ref) = refs[18 + n_prev:]

    b = pl.program_id(0)
    s = pl.program_id(1)
    n_ch = ts // DN_CHUNK
    x_in = _residual_in(x_ref, prev, ts)
    hn = _rms_mod(x_in, nmix_ref[...], mod_ref[1:2, :], mod_ref[0:1, :])
    hb = hn.astype(BF16)

    @pl.when(s == 0)
    def _():
        ext_ref[0:DN_HALO, :] = jnp.zeros((DN_HALO, 3 * DN_WIDTH), F32)
        state_ref[...] = jnp.zeros_like(state_ref)

    @pl.when(s > 0)
    def _():
        ext_ref[0:DN_HALO, :] = ext_ref[ts:ts + DN_HALO, :]

    for g in range(3):
        ext_ref[DN_HALO:DN_HALO + ts, g * DN_WIDTH:(g + 1) * DN_WIDTH] = _dot(
            hb, win_ref[:, g * DN_WIDTH:(g + 1) * DN_WIDTH])
    z = _dot(hb, win_ref[:, 3 * DN_WIDTH:4 * DN_WIDTH])
    for h in range(DN_HEADS):
        z_ref[h] = z[:, h * DN_DIM:(h + 1) * DN_DIM]

    for g in range(3):
        for h in range(DN_HEADS):
            c0 = g * DN_WIDTH + h * DN_DIM
            acc = None
            for k in range(DN_CONV_K):
                off = DN_HALO - (DN_CONV_K - 1) + k
                term = convw_ref[k:k + 1, c0:c0 + DN_DIM] * ext_ref[off:off + ts, c0:c0 + DN_DIM]
                acc = term if acc is None else acc + term
            cv = jax.nn.silu(acc)
            if g == 0:
                q_ref[h] = cv * lax.rsqrt(jnp.sum(cv * cv, axis=-1, keepdims=True) + EPS) * (DN_DIM ** -0.5)
            elif g == 1:
                k_ref[h] = cv * lax.rsqrt(jnp.sum(cv * cv, axis=-1, keepdims=True) + EPS)
            else:
                v_ref[h] = cv

    ba = _dot_split(hn, wba_ref)
    beta = jax.nn.sigmoid(ba)
    gdec = -jnp.exp(alog_ref[...]) * jax.nn.softplus(ba + dtb_ref[...])
    r = lax.broadcasted_iota(jnp.int32, (ts, ts), 0)
    c = lax.broadcasted_iota(jnp.int32, (ts, ts), 1)
    same_chunk = (r // DN_CHUNK) == (c // DN_CHUNK)
    tri = ((r >= c) & same_chunk).astype(BF16)
    g_hi = gdec.astype(BF16)
    g_r1 = gdec - g_hi.astype(F32)
    g_mid = g_r1.astype(BF16)
    g_lo = (g_r1 - g_mid.astype(F32)).astype(BF16)
    gc = _dot(tri, g_hi) + (_dot(tri, g_mid) + _dot(tri, g_lo))
    gct = gc.T
    for h in range(DN_HEADS):
        gcol_ref[h] = jnp.broadcast_to(gc[:, DN_HEADS + h:DN_HEADS + h + 1], (ts, LANES))
        bcol_ref[h] = jnp.broadcast_to(beta[:, h:h + 1], (ts, LANES))
        grow_ref[h] = jnp.broadcast_to(gct[DN_HEADS + h:DN_HEADS + h + 1, :], (SUBLANES, ts))

    ri = lax.broadcasted_iota(jnp.int32, (DN_CHUNK, DN_CHUNK), 0)
    ci = lax.broadcasted_iota(jnp.int32, (DN_CHUNK, DN_CHUNK), 1)
    causal = ri >= ci
    strict = ri > ci
    eye = (ri == ci).astype(F32)

    ci_pair = lax.broadcasted_iota(jnp.int32, (DN_CHUNK, 2 * DN_CHUNK), 1) & (DN_CHUNK - 1)
    ri_pair = lax.broadcasted_iota(jnp.int32, (DN_CHUNK, 2 * DN_CHUNK), 0)
    eye_pair = (ri_pair == ci_pair).astype(F32)
    zero_blk = jnp.zeros((DN_CHUNK, DN_CHUNK), BF16)

    def block_diag(p):
        return jnp.concatenate([jnp.concatenate([p[:, :DN_CHUNK], zero_blk], axis=1),
                                jnp.concatenate([zero_blk, p[:, DN_CHUNK:]], axis=1)], axis=0)

    def chunk_prep(h, ch):
        r0 = ch * DN_CHUNK
        q = q_ref[h, r0:r0 + DN_CHUNK, :]
        k = k_ref[h, r0:r0 + DN_CHUNK, :]
        gcl = gcol_ref[h, r0:r0 + DN_CHUNK, :]
        grw = grow_ref[h, 0:1, r0:r0 + DN_CHUNK]
        bt = bcol_ref[h, r0:r0 + DN_CHUNK, :]
        decay = jnp.where(causal, jnp.exp(jnp.where(causal, gcl - grw, 0.0)), 0.0)
        kbf = k.astype(BF16)
        nl = jnp.where(strict, -(_dot_nt((k * bt).astype(BF16), kbf) * decay), 0.0)
        attn = jnp.where(causal, _dot_nt(q.astype(BF16), kbf) * decay, 0.0)
        return nl, attn.astype(BF16)

    def state_free(h, ch, t_c):
        r0 = ch * DN_CHUNK
        q = q_ref[h, r0:r0 + DN_CHUNK, :]
        k = k_ref[h, r0:r0 + DN_CHUNK, :]
        v = v_ref[h, r0:r0 + DN_CHUNK, :]
        gcl = gcol_ref[h, r0:r0 + DN_CHUNK, :]
        bt = bcol_ref[h, r0:r0 + DN_CHUNK, :]
        eg = jnp.exp(gcl)
        rhs = jnp.concatenate([v * bt, k * bt * eg], axis=1).astype(BF16)
        uw = _dot(t_c.astype(BF16), rhs)
        glast = gcl[DN_CHUNK - 1:DN_CHUNK, :]
        kd = (k * jnp.exp(glast - gcl)).astype(BF16)
        return uw[:, :DN_DIM], uw[:, DN_DIM:].astype(BF16), (q * eg).astype(BF16), kd, jnp.exp(glast)

    def gate_out(h, ch, o):
        r0 = ch * DN_CHUNK
        zz = z_ref[h, r0:r0 + DN_CHUNK, :]
        o = o * lax.rsqrt(jnp.mean(o * o, axis=-1, keepdims=True) + EPS) * onorm_ref[...] * jax.nn.silu(zz)
        og_ref[h, r0:r0 + DN_CHUNK, :] = o

    def group_body(hg, carry):
        heads = [hg * DN_GROUP + g for g in range(DN_GROUP)]
        sts = [state_ref[h] for h in heads]
        for ch0 in range(0, n_ch, 2):
            preps = [(chunk_prep(h, ch0), chunk_prep(h, ch0 + 1)) for h in heads]
            nl2s = [jnp.concatenate([p[0][0], p[1][0]], axis=1) for p in preps]
            tms = [eye_pair + jnp.where((ri_pair >> 1) == (ci_pair >> 1), nl2, 0.0) for nl2 in nl2s]
            blk = 2
            while blk < DN_CHUNK:
                sel = (((ri_pair // (2 * blk)) == (ci_pair // (2 * blk)))
                       & ((ri_pair & (2 * blk - 1)) >= blk) & ((ci_pair & (2 * blk - 1)) < blk))
                tmbs = [tm.astype(BF16) for tm in tms]
                ys = [_dot(jnp.where(sel, nl2, 0.0).astype(BF16), block_diag(tmb))
                      for nl2, tmb in zip(nl2s, tmbs)]
                tms = [tm + _dot(tmb, block_diag(y.astype(BF16))) for tm, tmb, y in zip(tms, tmbs, ys)]
                blk *= 2
            free = [[state_free(h, ch0 + idx, tm[:, idx * DN_CHUNK:(idx + 1) * DN_CHUNK])
                     for h, tm in zip(heads, tms)] for idx in range(2)]
            for idx in range(2):
                sbs = [st.astype(BF16) for st in sts]
                w_s = [_dot(f[1], sb) for f, sb in zip(free[idx], sbs)]
                q_s = [_dot(f[2], sb) for f, sb in zip(free[idx], sbs)]
                vnbs = [(f[0] - ws).astype(BF16) for f, ws in zip(free[idx], w_s)]
                a_v = [_dot(p[idx][1], vnb) for p, vnb in zip(preps, vnbs)]
                k_v = [_dot_tn(f[3], vnb) for f, vnb in zip(free[idx], vnbs)]
                sts = [st * f[4] + kv for st, f, kv in zip(sts, free[idx], k_v)]
                for h, qs, av in zip(heads, q_s, a_v):
                    gate_out(h, ch0 + idx, qs + av)
        for h, st in zip(heads, sts):
            state_ref[h] = st
        return carry

    if DN_HEADS == DN_GROUP:
        group_body(0, 0)
    else:
        lax.fori_loop(0, DN_HEADS // DN_GROUP, group_body, 0)

    for h in range(DN_HEADS):
        mix_ref[:, h * DN_DIM:(h + 1) * DN_DIM] = og_ref[h].astype(BF16)
    y = _dot(mix_ref[...], wout_ref[...])
    x_new = x_in + mod_ref[2:3, :] * y
    xo_ref[...] = x_new
    _ffn_prep(x_new, (b == 0) & (s == 0), mod_ref, nffn_ref, wr_ref, br_ref, cnt_ref, hn2_ref, ids_ref,
              gates_ref, counts_ref, ts)


def _odd_layer(x, prev, mod_l, nmix, nffn, w_in, conv_w, a_log, dt_bias, onorm, w_out, w_r, b_r):
    ts = TS_ODD
    w_main = w_in[:, :4 * DN_WIDTH].astype(BF16)
    w_ba = _split_hi_lo(jnp.pad(w_in[:, 4 * DN_WIDTH:], ((0, 0), (0, LANES - 2 * DN_HEADS))))
    lane_pad = (DN_HEADS, LANES - 2 * DN_HEADS)
    params = [nmix.reshape(1, -1), nffn.reshape(1, -1), w_main, w_ba,
              jnp.pad(conv_w, ((0, SUBLANES - DN_CONV_K), (0, 0))),
              jnp.pad(a_log, lane_pad).reshape(1, -1), jnp.pad(dt_bias, lane_pad).reshape(1, -1),
              onorm.reshape(1, -1), w_out.astype(BF16), w_r, b_r]
    head = (DN_HEADS, ts, DN_DIM)
    scratch = [pltpu.VMEM((DN_HALO + ts, 3 * DN_WIDTH), F32),
               pltpu.VMEM(head, F32), pltpu.VMEM(head, F32), pltpu.VMEM(head, F32), pltpu.VMEM(head, F32),
               pltpu.VMEM(head, F32), pltpu.VMEM(head, F32), pltpu.VMEM((DN_HEADS, SUBLANES, ts), F32),
               pltpu.VMEM(head, F32), pltpu.VMEM((DN_HEADS, DN_DIM, DN_DIM), F32),
               pltpu.VMEM((ts, D_MODEL), BF16), pltpu.VMEM((SUBLANES, LANES), F32)]
    return _mixer_call(_odd_kernel, ts, x, prev, mod_l, params, scratch, "deltanet_mixer")


def _expert_kernel(n_slots, blk_e_ref, tok_ref, tok_next_ref, slot_ref, hn2_ref, wup_ref, wdn_ref, yt_ref,
                   xg_ref, ys_ref, xb_ref, sem_in, sem_out):
    i = pl.program_id(0)
    n = pl.num_programs(0)
    cur = i % 2

    def gather_copy(tok, r, buf):
        src = hn2_ref.at[pl.ds(pl.multiple_of(tok * ROW_TILES, ROW_TILES), ROW_TILES), :]
        return pltpu.make_async_copy(src, xg_ref.at[buf, pl.ds(r * ROW_TILES, ROW_TILES), :], sem_in.at[buf])

    def scatter_copy(slot, r, buf):
        dst = yt_ref.at[pl.ds(pl.multiple_of(slot * ROW_TILES, ROW_TILES), ROW_TILES), :]
        return pltpu.make_async_copy(ys_ref.at[buf, pl.ds(r * ROW_TILES, ROW_TILES), :], dst, sem_out.at[buf])

    block_rows = pl.ds(0, MOE_BLOCK * ROW_TILES)

    def wait_gathers(buf):
        pltpu.make_async_copy(hn2_ref.at[block_rows, :], xg_ref.at[buf], sem_in.at[buf]).wait()

    def wait_scatters(buf):
        pltpu.make_async_copy(ys_ref.at[buf], yt_ref.at[block_rows, :], sem_out.at[buf]).wait()

    def prime_copy(buf):
        spare = pl.ds((n_slots + buf * MOE_BLOCK) * ROW_TILES, MOE_BLOCK * ROW_TILES)
        return pltpu.make_async_copy(ys_ref.at[buf], yt_ref.at[spare, :], sem_out.at[buf])

    @pl.when(i == 0)
    def _():
        ys_ref[...] = jnp.zeros_like(ys_ref)
        prime_copy(0).start()
        prime_copy(1).start()
        for r in range(MOE_BLOCK):
            gather_copy(tok_ref[0, r], r, 0).start()

    for r in range(MOE_BLOCK):
        gather_copy(tok_next_ref[0, r], r, 1 - cur).start(priority=r % 2)

    wait_gathers(cur)
    for j in range(ROW_TILES):
        xb_ref[:, j * LANES:(j + 1) * LANES] = xg_ref[cur, pl.ds(j, MOE_BLOCK, stride=ROW_TILES), :].astype(BF16)
    gu = _dot(xb_ref[...], wup_ref[...])
    hid = jax.nn.silu(gu[:, :D_EXPERT]) * gu[:, D_EXPERT:]
    y = _dot(hid.astype(BF16), wdn_ref[...])

    wait_scatters(cur)
    for j in range(ROW_TILES):
        ys_ref[cur, pl.ds(j, MOE_BLOCK, stride=ROW_TILES), :] = y[:, j * LANES:(j + 1) * LANES]
    for r in range(MOE_BLOCK):
        scatter_copy(slot_ref[0, r], r, cur).start(priority=r % 2)

    @pl.when(i == n - 1)
    def _():
        wait_scatters(cur)
        wait_scatters(1 - cur)
        wait_gathers(1 - cur)


def _expert_mlp(hn2, buf_tok, buf_slot, blk_e, n_slots, w_up, w_down):
    n_blk = blk_e.shape[0]
    idx_block = (None, 1, MOE_BLOCK)
    grid_spec = pltpu.PrefetchScalarGridSpec(
        num_scalar_prefetch=1,
        grid=(n_blk,),
        in_specs=[pl.BlockSpec(idx_block, lambda i, be: (i, 0, 0), memory_space=pltpu.SMEM),
                  pl.BlockSpec(idx_block, lambda i, be: (jnp.minimum(i + 1, n_blk - 1), 0, 0),
                               memory_space=pltpu.SMEM),
                  pl.BlockSpec(idx_block, lambda i, be: (i, 0, 0), memory_space=pltpu.SMEM),
                  pl.BlockSpec(memory_space=pl.ANY),
                  pl.BlockSpec((None, D_MODEL, 2 * D_EXPERT), lambda i, be: (be[i], 0, 0)),
                  pl.BlockSpec((None, D_EXPERT, D_MODEL), lambda i, be: (be[i], 0, 0))],
        out_specs=pl.BlockSpec(memory_space=pl.ANY),
        scratch_shapes=[pltpu.VMEM((2, MOE_BLOCK * ROW_TILES, LANES), F32),
                        pltpu.VMEM((2, MOE_BLOCK * ROW_TILES, LANES), F32),
                        pltpu.VMEM((MOE_BLOCK, D_MODEL), BF16),
                        pltpu.SemaphoreType.DMA((2,)), pltpu.SemaphoreType.DMA((2,))],
    )
    tok3 = buf_tok.reshape(n_blk, 1, MOE_BLOCK)
    return pl.pallas_call(
        functools.partial(_expert_kernel, n_slots),
        grid_spec=grid_spec,
        out_shape=jax.ShapeDtypeStruct(((n_slots + 2 * MOE_BLOCK) * ROW_TILES, LANES), F32),
        compiler_params=pltpu.CompilerParams(dimension_semantics=("arbitrary",), vmem_limit_bytes=VMEM_LIMIT),
        name="expert_mlp",
    )(blk_e, tok3, tok3, buf_slot.reshape(n_blk, 1, MOE_BLOCK), hn2, w_up, w_down)


def _moe(hn2, ids, counts, w_up, w_down):
    n_tok = ids.shape[0]
    m = n_tok * TOP_K
    p_rows = (m + N_EXPERTS * (MOE_BLOCK - 1) + MOE_BLOCK - 1) // MOE_BLOCK * MOE_BLOCK
    n_blk = p_rows // MOE_BLOCK
    eid = ids[:, 0:TOP_K]
    rank = ids[:, TOP_K:2 * TOP_K]
    cnt = counts[0, ROUTER_E0:ROUTER_E0 + N_EXPERTS].astype(jnp.int32)
    padded = (cnt + MOE_BLOCK - 1) // MOE_BLOCK * MOE_BLOCK
    pend = jnp.cumsum(padded)
    pstart = pend - padded
    dest = (pstart[eid] + rank).reshape(-1)
    pair_of_row = jnp.full((p_rows,), -1, jnp.int32).at[dest].set(jnp.arange(m, dtype=jnp.int32))
    is_pad = pair_of_row < 0
    pad_rank = jnp.cumsum(is_pad.astype(jnp.int32)) - 1
    tok_of_row = pair_of_row // TOP_K
    buf_slot = jnp.where(is_pad, m + pad_rank, (pair_of_row % TOP_K) * n_tok + tok_of_row)
    buf_tok = jnp.where(is_pad, 0, tok_of_row)
    blk_start = jnp.arange(n_blk, dtype=jnp.int32) * MOE_BLOCK
    blk_e = jnp.minimum(jnp.sum(pend[None, :] <= blk_start[:, None], axis=1), N_EXPERTS - 1).astype(jnp.int32)
    return _expert_mlp(hn2, buf_tok, buf_slot, blk_e, p_rows, w_up, w_down)


def _final_kernel(ts, x_ref, yt0_ref, yt1_ref, pg_ref, pmod_ref, gain_ref, o_ref):
    x = _residual_in(x_ref, (yt0_ref, yt1_ref, pg_ref, pmod_ref), ts)
    ms = jnp.mean(x * x, axis=-1, keepdims=True)
    o_ref[...] = x * lax.rsqrt(ms + EPS) * gain_ref[...]


def _final(x, prev, gain):
    yt, pgates, pmod = prev
    ts = TS_EVEN
    n_tok = x.shape[0]
    bsz = pmod.shape[0]
    n_s = n_tok // bsz // ts
    tok_map = lambda b, s: (b * n_s + s, 0)
    return pl.pallas_call(
        functools.partial(_final_kernel, ts),
        grid=(bsz, n_s),
        in_specs=[pl.BlockSpec((ts, D_MODEL), tok_map),
                  pl.BlockSpec((ts * ROW_TILES, LANES), tok_map),
                  pl.BlockSpec((ts * ROW_TILES, LANES), lambda b, s: (n_tok // ts + b * n_s + s, 0)),
                  pl.BlockSpec((ts, LANES), tok_map),
                  pl.BlockSpec((None, SUBLANES, D_MODEL), lambda b, s: (b, 0, 0)),
                  pl.BlockSpec((1, D_MODEL), lambda b, s: (0, 0))],
        out_specs=pl.BlockSpec((ts, D_MODEL), tok_map),
        out_shape=jax.ShapeDtypeStruct((n_tok, D_MODEL), F32),
        compiler_params=pltpu.CompilerParams(dimension_semantics=("arbitrary", "arbitrary"),
                                             vmem_limit_bytes=VMEM_LIMIT),
        name="final_norm",
    )(x, yt, yt, pgates, pmod, gain.reshape(1, -1))


def kernel(x, c, mod_w, mod_b, norm_mix, norm_ffn, ab_w_in, pool_w, pool_scale, conv_w, conv_b, conv_ln_g,
           conv_ln_b, ab_w_out, dn_w_in, dn_conv_w, dn_a_log, dn_dt_bias, dn_onorm, dn_w_out, moe_w_grp,
           moe_b_grp, moe_w_exp, moe_b_exp, moe_w_up, moe_w_down, final_norm):
    bsz, seq, d = x.shape
    depth = mod_w.shape[0]
    assert d == D_MODEL and seq % TS_EVEN == 0 and seq % TS_ODD == 0
    mod = _modulation(c, mod_w, mod_b)
    xt = x.reshape(bsz * seq, d)
    prev = None
    for l in range(depth):
        i = l // 2
        w_r = _split_hi_lo(jnp.pad(jnp.concatenate([moe_w_grp[l], moe_w_exp[l]], axis=1),
                                   ((0, 0), (0, LANES - N_GROUPS - N_EXPERTS))))
        b_r = jnp.pad(jnp.concatenate([moe_b_grp[l], moe_b_exp[l]]), (0, LANES - N_GROUPS - N_EXPERTS))
        b_r = b_r.reshape(1, -1)
        if l % 2 == 0:
            outs = _even_layer(xt, prev, mod[l], norm_mix[l], norm_ffn[l], ab_w_in[i], pool_w[i], pool_scale[i],
                               conv_w[i], conv_b[i], conv_ln_g[i], conv_ln_b[i], ab_w_out[i], w_r, b_r)
        else:
            outs = _odd_layer(xt, prev, mod[l], norm_mix[l], norm_ffn[l], dn_w_in[i], dn_conv_w[i], dn_a_log[i],
                              dn_dt_bias[i], dn_onorm[i], dn_w_out[i], w_r, b_r)
        xt, hn2, ids, gates, counts = outs
        yt = _moe(hn2, ids, counts, moe_w_up[l].astype(BF16), moe_w_down[l].astype(BF16))
        prev = (yt, gates, mod[l])
    out = _final(xt, prev, final_norm)
    return out.reshape(bsz, seq, d)
```

```python
import functools

import jax
import jax.numpy as jnp
from jax import lax
from jax.experimental import pallas as pl
from jax.experimental.pallas import tpu as pltpu

F32 = jnp.float32
BF16 = jnp.bfloat16
HI = lax.Precision.HIGHEST

D_MODEL = 1024
EPS = 1e-6
LANES = 128
SUBLANES = 8
ROW_TILES = D_MODEL // LANES
VMEM_LIMIT = 56 * 1024 * 1024

POOL_WINDOWS = (2, 4, 8, 16)
POOL_GROUP = 128
POOL_WIDTH = 512
CONV_WIDTH = 512
CONV_K = 31
AB_IN = POOL_WIDTH + 2 * CONV_WIDTH
HALO = 32

DN_HEADS = 8
DN_DIM = 128
DN_WIDTH = DN_HEADS * DN_DIM
DN_CONV_K = 4
DN_CHUNK = 128
DN_HALO = 8
DN_GROUP = 8
DN_INVERT = 8

N_GROUPS = 4
EPG = 8
N_EXPERTS = N_GROUPS * EPG
TOP_K = 2
D_EXPERT = 256
MOE_BLOCK = 256
ROUTER_E0 = N_GROUPS

TS_EVEN = 512
TS_ODD = 256

NEG = -1e30


def _dot(a, b):
    return jnp.dot(a, b, preferred_element_type=F32)


def _dot_nt(a, b):
    return lax.dot_general(a, b, (((1,), (1,)), ((), ())), preferred_element_type=F32)


def _dot_tn(a, b):
    return lax.dot_general(a, b, (((0,), (0,)), ((), ())), preferred_element_type=F32)


def _split_hi_lo(w):
    hi = w.astype(BF16)
    return jnp.stack([hi, (w - hi.astype(F32)).astype(BF16)])


def _dot_split(x, w_ref):
    xh = x.astype(BF16)
    xl = (x - xh.astype(F32)).astype(BF16)
    return _dot(xh, w_ref[0]) + (_dot(xl, w_ref[0]) + _dot(xh, w_ref[1]))


def _rms_mod(x, gain, scale, shift):
    ms = jnp.mean(x * x, axis=-1, keepdims=True)
    return x * lax.rsqrt(ms + EPS) * gain * (1.0 + scale) + shift


def _load_rows(ref, n, first, stride):
    return jnp.concatenate([ref[pl.ds(first + j, n, stride=stride), :] for j in range(ROW_TILES)], axis=1)


def _store_rows(ref, val, n):
    for j in range(ROW_TILES):
        ref[pl.ds(j, n, stride=ROW_TILES), :] = val[:, j * LANES:(j + 1) * LANES]


def _mod_kernel(c_ref, w_ref, b_ref, o_ref):
    ca = jax.nn.silu(c_ref[...])
    o_ref[...] = jnp.dot(ca, w_ref[...], precision=HI, preferred_element_type=F32) + b_ref[...]


def _modulation(c, mod_w, mod_b):
    depth = mod_w.shape[0]
    bsz = c.shape[0]
    n_col = mod_w.shape[2] // D_MODEL
    out = pl.pallas_call(
        _mod_kernel,
        grid=(depth, n_col),
        in_specs=[
            pl.BlockSpec((bsz, D_MODEL), lambda l, j: (0, 0)),
            pl.BlockSpec((None, D_MODEL, D_MODEL), lambda l, j: (l, 0, j)),
            pl.BlockSpec((None, 1, D_MODEL), lambda l, j: (l, 0, j)),
        ],
        out_specs=pl.BlockSpec((None, bsz, D_MODEL), lambda l, j: (l, 0, j)),
        out_shape=jax.ShapeDtypeStruct((depth, bsz, n_col * D_MODEL), F32),
        compiler_params=pltpu.CompilerParams(dimension_semantics=("arbitrary", "arbitrary"),
                                             vmem_limit_bytes=VMEM_LIMIT),
        name="modulation",
    )(c, mod_w, mod_b.reshape(depth, 1, n_col * D_MODEL))
    mod = out.reshape(depth, bsz, n_col, D_MODEL)
    return jnp.pad(mod, ((0, 0), (0, 0), (0, SUBLANES - n_col), (0, 0)))


def _residual_in(x_ref, prev, ts):
    x = x_ref[...]
    if prev is None:
        return x
    yt0_ref, yt1_ref, pg_ref, pmod_ref = prev
    y0 = _load_rows(yt0_ref, ts, 0, ROW_TILES)
    y1 = _load_rows(yt1_ref, ts, 0, ROW_TILES)
    pg = pg_ref[...]
    return x + pmod_ref[5:6, :] * (y0 * pg[:, 0:1] + y1 * pg[:, 1:2])


def _ffn_prep(x_new, first_step, mod_ref, nffn_ref, wr_ref, br_ref, cnt_ref, hn2_ref, ids_ref, gates_ref,
              counts_ref, ts):
    hn2 = _rms_mod(x_new, nffn_ref[...], mod_ref[4:5, :], mod_ref[3:4, :])
    _store_rows(hn2_ref, hn2, ts)
    lg = _dot_split(hn2, wr_ref) + br_ref[...]
    lane = lax.broadcasted_iota(jnp.int32, (ts, LANES), 1)
    is_grp = lane < N_GROUPS
    gl = jnp.where(is_grp, lg, NEG)
    gmax = jnp.max(gl, axis=-1, keepdims=True)
    gsum = jnp.sum(jnp.where(is_grp, jnp.exp(gl - gmax), 0.0), axis=-1, keepdims=True)
    gp = 1.0 / gsum
    gi = jnp.min(jnp.where(gl == gmax, lane, LANES), axis=-1, keepdims=True)
    lo = ROUTER_E0 + gi * EPG
    in_grp = (lane >= lo) & (lane < lo + EPG)
    el = jnp.where(in_grp, lg, NEG)
    m1 = jnp.max(el, axis=-1, keepdims=True)
    i1 = jnp.min(jnp.where(el == m1, lane, LANES), axis=-1, keepdims=True)
    el2 = jnp.where(lane == i1, NEG, el)
    m2 = jnp.max(el2, axis=-1, keepdims=True)
    i2 = jnp.min(jnp.where(el2 == m2, lane, LANES), axis=-1, keepdims=True)
    e2 = jnp.exp(m2 - m1)
    w1 = gp / (1.0 + e2)
    w2 = gp * e2 / (1.0 + e2)

    @pl.when(first_step)
    def _():
        cnt_ref[...] = jnp.zeros_like(cnt_ref)

    onehot = ((lane == i1) | (lane == i2)).astype(F32)
    r = lax.broadcasted_iota(jnp.int32, (ts, ts), 0)
    c = lax.broadcasted_iota(jnp.int32, (ts, ts), 1)
    before = (r > c).astype(BF16)
    run = _dot(before, onehot.astype(BF16)) + cnt_ref[0:1, :]
    rank1 = jnp.sum(jnp.where(lane == i1, run, 0.0), axis=-1, keepdims=True)
    rank2 = jnp.sum(jnp.where(lane == i2, run, 0.0), axis=-1, keepdims=True)
    total = cnt_ref[0:1, :] + jnp.sum(onehot, axis=0, keepdims=True)
    cnt_ref[...] = jnp.broadcast_to(total, cnt_ref.shape)
    counts_ref[...] = jnp.broadcast_to(total, counts_ref.shape)

    ids = jnp.where(lane == 0, i1 - ROUTER_E0,
                    jnp.where(lane == 1, i2 - ROUTER_E0,
                              jnp.where(lane == 2, rank1.astype(jnp.int32),
                                        jnp.where(lane == 3, rank2.astype(jnp.int32), 0))))
    ids_ref[...] = ids
    gates_ref[...] = jnp.where(lane == 0, w1, jnp.where(lane == 1, w2, 0.0))


def _mixer_call(body, ts, x, prev, mod_l, params, scratch, name):
    n_tok = x.shape[0]
    bsz = mod_l.shape[0]
    seq = n_tok // bsz
    n_s = seq // ts
    tok_map = lambda b, s: (b * n_s + s, 0)
    in_specs = [pl.BlockSpec((ts, D_MODEL), tok_map)]
    args = [x]
    if prev is not None:
        yt, pgates, pmod = prev
        n_tiles = n_tok // ts
        in_specs += [pl.BlockSpec((ts * ROW_TILES, LANES), tok_map),
                     pl.BlockSpec((ts * ROW_TILES, LANES), lambda b, s: (n_tiles + b * n_s + s, 0)),
                     pl.BlockSpec((ts, LANES), tok_map),
                     pl.BlockSpec((None, SUBLANES, D_MODEL), lambda b, s: (b, 0, 0))]
        args += [yt, yt, pgates, pmod]
    in_specs.append(pl.BlockSpec((None, SUBLANES, D_MODEL), lambda b, s: (b, 0, 0)))
    args.append(mod_l)
    for arr in params:
        nd = arr.ndim
        in_specs.append(pl.BlockSpec(arr.shape, lambda b, s, _nd=nd: (0,) * _nd))
        args.append(arr)
    out_shape = [
        jax.ShapeDtypeStruct((n_tok, D_MODEL), F32),
        jax.ShapeDtypeStruct((n_tok * ROW_TILES, LANES), F32),
        jax.ShapeDtypeStruct((n_tok, LANES), jnp.int32),
        jax.ShapeDtypeStruct((n_tok, LANES), F32),
        jax.ShapeDtypeStruct((SUBLANES, LANES), F32),
    ]
    out_specs = [
        pl.BlockSpec((ts, D_MODEL), tok_map),
        pl.BlockSpec((ts * ROW_TILES, LANES), tok_map),
        pl.BlockSpec((ts, LANES), tok_map),
        pl.BlockSpec((ts, LANES), tok_map),
        pl.BlockSpec((SUBLANES, LANES), lambda b, s: (0, 0)),
    ]
    return pl.pallas_call(
        functools.partial(body, prev is not None, ts),
        grid=(bsz, n_s),
        in_specs=in_specs,
        out_specs=out_specs,
        out_shape=out_shape,
        scratch_shapes=scratch,
        compiler_params=pltpu.CompilerParams(dimension_semantics=("arbitrary", "arbitrary"),
                                             vmem_limit_bytes=VMEM_LIMIT),
        name=name,
    )(*args)


def _even_kernel(has_prev, ts, *refs):
    n_prev = 4 if has_prev else 0
    x_ref = refs[0]
    prev = refs[1:1 + n_prev] if has_prev else None
    (mod_ref, nmix_ref, nffn_ref, win_ref, poolw_ref, pscale_ref, convw_ref, convb_ref, lng_ref, lnb_ref,
     wout_ref, wr_ref, br_ref) = refs[1 + n_prev:14 + n_prev]
    xo_ref, hn2_ref, ids_ref, gates_ref, counts_ref = refs[14 + n_prev:19 + n_prev]
    extp_ref, extc_ref, shift_ref, convo_ref, mix_ref, cnt_ref = refs[19 + n_prev:]

    b = pl.program_id(0)
    s = pl.program_id(1)
    x_in = _residual_in(x_ref, prev, ts)
    hn = _rms_mod(x_in, nmix_ref[...], mod_ref[1:2, :], mod_ref[0:1, :])
    u = _dot(hn.astype(BF16), win_ref[...])

    @pl.when(s == 0)
    def _():
        extp_ref[0:HALO, :] = jnp.zeros((HALO, POOL_WIDTH), F32)
        extc_ref[0:HALO, :] = jnp.zeros((HALO, CONV_WIDTH), F32)

    @pl.when(s > 0)
    def _():
        extp_ref[0:HALO, :] = extp_ref[ts:ts + HALO, :]
        extc_ref[0:HALO, :] = extc_ref[ts:ts + HALO, :]

    extp_ref[HALO:HALO + ts, :] = u[:, :POOL_WIDTH]
    extc_ref[HALO:HALO + ts, :] = (u[:, POOL_WIDTH:POOL_WIDTH + CONV_WIDTH]
                                   * jax.nn.sigmoid(u[:, POOL_WIDTH + CONV_WIDTH:]))

    pos = (s * ts + 1 + lax.broadcasted_iota(jnp.int32, (ts, 1), 0)).astype(F32)
    for j, w in enumerate(POOL_WINDOWS):
        c0 = j * POOL_GROUP
        a = extp_ref[HALO:HALO + ts, c0:c0 + POOL_GROUP]
        win = a
        for d in range(1, w):
            win = win + extp_ref[HALO - d:HALO - d + ts, c0:c0 + POOL_GROUP]
        pooled = win / jnp.minimum(pos, float(w)) - a
        ya = _dot(pooled.astype(BF16), poolw_ref[j]) * pscale_ref[:, c0:c0 + POOL_GROUP]
        mix_ref[:, c0:c0 + POOL_GROUP] = ya.astype(BF16)

    for sh in range(1, SUBLANES):
        shift_ref[sh - 1, SUBLANES:HALO + ts, :] = extc_ref[SUBLANES - sh:HALO + ts - sh, :]
    rc = 32

    def conv_chunk(ci, carry):
        base = pl.multiple_of(ci * rc, rc)
        acc = jnp.broadcast_to(convb_ref[...], (rc, CONV_WIDTH))
        for k in range(CONV_K):
            delay = CONV_K - 1 - k
            a8, sh = delay // SUBLANES, delay % SUBLANES
            rows = pl.ds(base + (HALO - a8 * SUBLANES), rc)
            tap = extc_ref[rows, :] if sh == 0 else shift_ref[sh - 1, rows, :]
            acc = acc + convw_ref[k:k + 1, :] * tap
        convo_ref[pl.ds(base, rc), :] = acc
        return carry

    lax.fori_loop(0, ts // rc, conv_chunk, 0)
    cv = convo_ref[...]
    mu = jnp.mean(cv, axis=-1, keepdims=True)
    cen = cv - mu
    var = jnp.mean(cen * cen, axis=-1, keepdims=True)
    yb = cen * lax.rsqrt(var + EPS) * lng_ref[...] + lnb_ref[...]
    mix_ref[:, POOL_WIDTH:POOL_WIDTH + CONV_WIDTH] = jax.nn.silu(yb).astype(BF16)

    y = _dot(mix_ref[...], wout_ref[...])
    x_new = x_in + mod_ref[2:3, :] * y
    xo_ref[...] = x_new
    _ffn_prep(x_new, (b == 0) & (s == 0), mod_ref, nffn_ref, wr_ref, br_ref, cnt_ref, hn2_ref, ids_ref,
              gates_ref, counts_ref, ts)


def _even_layer(x, prev, mod_l, nmix, nffn, w_in, pool_w, pool_scale, conv_w, conv_b, ln_g, ln_b, w_out, w_r,
                b_r):
    ts = TS_EVEN
    params = [nmix.reshape(1, -1), nffn.reshape(1, -1), w_in.astype(BF16), pool_w.astype(BF16),
              pool_scale.reshape(1, -1), jnp.pad(conv_w, ((0, HALO - CONV_K), (0, 0))), conv_b.reshape(1, -1),
              ln_g.reshape(1, -1), ln_b.reshape(1, -1), w_out.astype(BF16), w_r, b_r]
    scratch = [pltpu.VMEM((HALO + ts, POOL_WIDTH), F32), pltpu.VMEM((HALO + ts, CONV_WIDTH), F32),
               pltpu.VMEM((SUBLANES - 1, HALO + ts, CONV_WIDTH), F32), pltpu.VMEM((ts, CONV_WIDTH), F32),
               pltpu.VMEM((ts, D_MODEL), BF16), pltpu.VMEM((SUBLANES, LANES), F32)]
    return _mixer_call(_even_kernel, ts, x, prev, mod_l, params, scratch, "even_mixer")


def _odd_kernel(has_prev, ts, *refs):
    n_prev = 4 if has_prev else 0
    x_ref = refs[0]
    prev = refs[1:1 + n_prev] if has_prev else None
    (mod_ref, nmix_ref, nffn_ref, win_ref, wba_ref, convw_ref, alog_ref, dtb_ref, onorm_ref, wout_ref, wr_ref,
     br_ref) = refs[1 + n_prev:13 + n_prev]
    xo_ref, hn2_ref, ids_ref, gates_ref, counts_ref = refs[13 + n_prev:18 + n_prev]
    (ext_ref, q_ref, k_ref, v_ref, z_ref, gcol_ref, bcol_ref, grow_ref, og_ref, state_ref, mix_ref,
     cnt_ref) = refs[18 + n_prev:]

    b = pl.program_id(0)
    s = pl.program_id(1)
    n_ch = ts // DN_CHUNK
    x_in = _residual_in(x_ref, prev, ts)
    hn = _rms_mod(x_in, nmix_ref[...], mod_ref[1:2, :], mod_ref[0:1, :])
    hb = hn.astype(BF16)

    @pl.when(s == 0)
    def _():
        ext_ref[0:DN_HALO, :] = jnp.zeros((DN_HALO, 3 * DN_WIDTH), F32)
        state_ref[...] = jnp.zeros_like(state_ref)

    @pl.when(s > 0)
    def _():
        ext_ref[0:DN_HALO, :] = ext_ref[ts:ts + DN_HALO, :]

    for g in range(3):
        ext_ref[DN_HALO:DN_HALO + ts, g * DN_WIDTH:(g + 1) * DN_WIDTH] = _dot(
            hb, win_ref[:, g * DN_WIDTH:(g + 1) * DN_WIDTH])
    z = _dot(hb, win_ref[:, 3 * DN_WIDTH:4 * DN_WIDTH])
    for h in range(DN_HEADS):
        z_ref[h] = z[:, h * DN_DIM:(h + 1) * DN_DIM]

    for g in range(3):
        for h in range(DN_HEADS):
            c0 = g * DN_WIDTH + h * DN_DIM
            acc = None
            for k in range(DN_CONV_K):
                off = DN_HALO - (DN_CONV_K - 1) + k
                term = convw_ref[k:k + 1, c0:c0 + DN_DIM] * ext_ref[off:off + ts, c0:c0 + DN_DIM]
                acc = term if acc is None else acc + term
            cv = jax.nn.silu(acc)
            if g == 0:
                q_ref[h] = cv * lax.rsqrt(jnp.sum(cv * cv, axis=-1, keepdims=True) + EPS) * (DN_DIM ** -0.5)
            elif g == 1:
                k_ref[h] = cv * lax.rsqrt(jnp.sum(cv * cv, axis=-1, keepdims=True) + EPS)
            else:
                v_ref[h] = cv

    ba = _dot_split(hn, wba_ref)
    beta = jax.nn.sigmoid(ba)
    gdec = -jnp.exp(alog_ref[...]) * jax.nn.softplus(ba + dtb_ref[...])
    r = lax.broadcasted_iota(jnp.int32, (ts, ts), 0)
    c = lax.broadcasted_iota(jnp.int32, (ts, ts), 1)
    same_chunk = (r // DN_CHUNK) == (c // DN_CHUNK)
    tri = ((r >= c) & same_chunk).astype(BF16)
    g_hi = gdec.astype(BF16)
    g_r1 = gdec - g_hi.astype(F32)
    g_mid = g_r1.astype(BF16)
    g_lo = (g_r1 - g_mid.astype(F32)).astype(BF16)
    gc = _dot(tri, g_hi) + (_dot(tri, g_mid) + _dot(tri, g_lo))
    gct = gc.T
    for h in range(DN_HEADS):
        gcol_ref[h] = jnp.broadcast_to(gc[:, DN_HEADS + h:DN_HEADS + h + 1], (ts, LANES))
        bcol_ref[h] = jnp.broadcast_to(beta[:, h:h + 1], (ts, LANES))
        grow_ref[h] = jnp.broadcast_to(gct[DN_HEADS + h:DN_HEADS + h + 1, :], (SUBLANES, ts))

    ri = lax.broadcasted_iota(jnp.int32, (DN_CHUNK, DN_CHUNK), 0)
    ci = lax.broadcasted_iota(jnp.int32, (DN_CHUNK, DN_CHUNK), 1)
    causal = ri >= ci
    strict = ri > ci
    eye = (ri == ci).astype(F32)

    ci_pair = lax.broadcasted_iota(jnp.int32, (DN_CHUNK, 2 * DN_CHUNK), 1) & (DN_CHUNK - 1)
    ri_pair = lax.broadcasted_iota(jnp.int32, (DN_CHUNK, 2 * DN_CHUNK), 0)
    eye_pair = (ri_pair == ci_pair).astype(F32)
    zero_blk = jnp.zeros((DN_CHUNK, DN_CHUNK), BF16)

    def block_diag(p):
        return jnp.concatenate([jnp.concatenate([p[:, :DN_CHUNK], zero_blk], axis=1),
                                jnp.concatenate([zero_blk, p[:, DN_CHUNK:]], axis=1)], axis=0)

    def chunk_prep(h, ch):
        r0 = ch * DN_CHUNK
        q = q_ref[h, r0:r0 + DN_CHUNK, :]
        k = k_ref[h, r0:r0 + DN_CHUNK, :]
        gcl = gcol_ref[h, r0:r0 + DN_CHUNK, :]
        grw = grow_ref[h, 0:1, r0:r0 + DN_CHUNK]
        bt = bcol_ref[h, r0:r0 + DN_CHUNK, :]
        decay = jnp.where(causal, jnp.exp(jnp.where(causal, gcl - grw, 0.0)), 0.0)
        kbf = k.astype(BF16)
        nl = jnp.where(strict, -(_dot_nt((k * bt).astype(BF16), kbf) * decay), 0.0)
        attn = jnp.where(causal, _dot_nt(q.astype(BF16), kbf) * decay, 0.0)
        return nl, attn.astype(BF16)

    def state_free(h, ch, t_c):
        r0 = ch * DN_CHUNK
        q = q_ref[h, r0:r0 + DN_CHUNK, :]
        k = k_ref[h, r0:r0 + DN_CHUNK, :]
        v = v_ref[h, r0:r0 + DN_CHUNK, :]
        gcl = gcol_ref[h, r0:r0 + DN_CHUNK, :]
        bt = bcol_ref[h, r0:r0 + DN_CHUNK, :]
        eg = jnp.exp(gcl)
        rhs = jnp.concatenate([v * bt, k * bt * eg], axis=1).astype(BF16)
        uw = _dot(t_c.astype(BF16), rhs)
        glast = gcl[DN_CHUNK - 1:DN_CHUNK, :]
        kd = (k * jnp.exp(glast - gcl)).astype(BF16)
        return uw[:, :DN_DIM], uw[:, DN_DIM:].astype(BF16), (q * eg).astype(BF16), kd, jnp.exp(glast)

    def gate_out(h, ch, o):
        r0 = ch * DN_CHUNK
        zz = z_ref[h, r0:r0 + DN_CHUNK, :]
        o = o * lax.rsqrt(jnp.mean(o * o, axis=-1, keepdims=True) + EPS) * onorm_ref[...] * jax.nn.silu(zz)
        og_ref[h, r0:r0 + DN_CHUNK, :] = o

    def group_body(hg, carry):
        heads = [hg * DN_GROUP + g for g in range(DN_GROUP)]
        sts = [state_ref[h] for h in heads]
        for ch0 in range(0, n_ch, 2):
            preps, tms = [], []
            for sub in range(0, DN_GROUP, DN_INVERT):
                sub_preps = [(chunk_prep(h, ch0), chunk_prep(h, ch0 + 1)) for h in heads[sub:sub + DN_INVERT]]
                nl2s = [jnp.concatenate([p[0][0], p[1][0]], axis=1) for p in sub_preps]
                sub_tms = [eye_pair + jnp.where((ri_pair >> 1) == (ci_pair >> 1), nl2, 0.0) for nl2 in nl2s]
                blk = 2
                while blk < DN_CHUNK:
                    sel = (((ri_pair // (2 * blk)) == (ci_pair // (2 * blk)))
                           & ((ri_pair & (2 * blk - 1)) >= blk) & ((ci_pair & (2 * blk - 1)) < blk))
                    tmbs = [tm.astype(BF16) for tm in sub_tms]
                    ys = [_dot(jnp.where(sel, nl2, 0.0).astype(BF16), block_diag(tmb))
                          for nl2, tmb in zip(nl2s, tmbs)]
                    sub_tms = [tm + _dot(tmb, block_diag(y.astype(BF16)))
                               for tm, tmb, y in zip(sub_tms, tmbs, ys)]
                    blk *= 2
                preps += sub_preps
                tms += sub_tms
            free = [[state_free(h, ch0 + idx, tm[:, idx * DN_CHUNK:(idx + 1) * DN_CHUNK])
                     for h, tm in zip(heads, tms)] for idx in range(2)]
            for idx in range(2):
                sbs = [st.astype(BF16) for st in sts]
                w_s = [_dot(f[1], sb) for f, sb in zip(free[idx], sbs)]
                q_s = [_dot(f[2], sb) for f, sb in zip(free[idx], sbs)]
                vnbs = [(f[0] - ws).astype(BF16) for f, ws in zip(free[idx], w_s)]
                a_v = [_dot(p[idx][1], vnb) for p, vnb in zip(preps, vnbs)]
                k_v = [_dot_tn(f[3], vnb) for f, vnb in zip(free[idx], vnbs)]
                sts = [st * f[4] + kv for st, f, kv in zip(sts, free[idx], k_v)]
                for h, qs, av in zip(heads, q_s, a_v):
                    gate_out(h, ch0 + idx, qs + av)
        for h, st in zip(heads, sts):
            state_ref[h] = st
        return carry

    if DN_HEADS == DN_GROUP:
        group_body(0, 0)
    else:
        lax.fori_loop(0, DN_HEADS // DN_GROUP, group_body, 0)

    for h in range(DN_HEADS):
        mix_ref[:, h * DN_DIM:(h + 1) * DN_DIM] = og_ref[h].astype(BF16)
    y = _dot(mix_ref[...], wout_ref[...])
    x_new = x_in + mod_ref[2:3, :] * y
    xo_ref[...] = x_new
    _ffn_prep(x_new, (b == 0) & (s == 0), mod_ref, nffn_ref, wr_ref, br_ref, cnt_ref, hn2_ref, ids_ref,
              gates_ref, counts_ref, ts)


def _odd_layer(x, prev, mod_l, nmix, nffn, w_in, conv_w, a_log, dt_bias, onorm, w_out, w_r, b_r):
    ts = TS_ODD
    w_main = w_in[:, :4 * DN_WIDTH].astype(BF16)
    w_ba = _split_hi_lo(jnp.pad(w_in[:, 4 * DN_WIDTH:], ((0, 0), (0, LANES - 2 * DN_HEADS))))
    lane_pad = (DN_HEADS, LANES - 2 * DN_HEADS)
    params = [nmix.reshape(1, -1), nffn.reshape(1, -1), w_main, w_ba,
              jnp.pad(conv_w, ((0, SUBLANES - DN_CONV_K), (0, 0))),
              jnp.pad(a_log, lane_pad).reshape(1, -1), jnp.pad(dt_bias, lane_pad).reshape(1, -1),
              onorm.reshape(1, -1), w_out.astype(BF16), w_r, b_r]
    head = (DN_HEADS, ts, DN_DIM)
    scratch = [pltpu.VMEM((DN_HALO + ts, 3 * DN_WIDTH), F32),
               pltpu.VMEM(head, F32), pltpu.VMEM(head, F32), pltpu.VMEM(head, F32), pltpu.VMEM(head, F32),
               pltpu.VMEM(head, F32), pltpu.VMEM(head, F32), pltpu.VMEM((DN_HEADS, SUBLANES, ts), F32),
               pltpu.VMEM(head, F32), pltpu.VMEM((DN_HEADS, DN_DIM, DN_DIM), F32),
               pltpu.VMEM((ts, D_MODEL), BF16), pltpu.VMEM((SUBLANES, LANES), F32)]
    return _mixer_call(_odd_kernel, ts, x, prev, mod_l, params, scratch, "deltanet_mixer")


def _expert_kernel(n_slots, blk_e_ref, tok_ref, tok_next_ref, slot_ref, hn2_ref, wup_ref, wdn_ref, yt_ref,
                   xg_ref, ys_ref, xb_ref, sem_in, sem_out):
    i = pl.program_id(0)
    n = pl.num_programs(0)
    cur = i % 2

    def gather_copy(tok, r, buf):
        src = hn2_ref.at[pl.ds(pl.multiple_of(tok * ROW_TILES, ROW_TILES), ROW_TILES), :]
        return pltpu.make_async_copy(src, xg_ref.at[buf, pl.ds(r * ROW_TILES, ROW_TILES), :], sem_in.at[buf])

    def scatter_copy(slot, r, buf):
        dst = yt_ref.at[pl.ds(pl.multiple_of(slot * ROW_TILES, ROW_TILES), ROW_TILES), :]
        return pltpu.make_async_copy(ys_ref.at[buf, pl.ds(r * ROW_TILES, ROW_TILES), :], dst, sem_out.at[buf])

    block_rows = pl.ds(0, MOE_BLOCK * ROW_TILES)

    def wait_gathers(buf):
        pltpu.make_async_copy(hn2_ref.at[block_rows, :], xg_ref.at[buf], sem_in.at[buf]).wait()

    def wait_scatters(buf):
        pltpu.make_async_copy(ys_ref.at[buf], yt_ref.at[block_rows, :], sem_out.at[buf]).wait()

    def prime_copy(buf):
        spare = pl.ds((n_slots + buf * MOE_BLOCK) * ROW_TILES, MOE_BLOCK * ROW_TILES)
        return pltpu.make_async_copy(ys_ref.at[buf], yt_ref.at[spare, :], sem_out.at[buf])

    @pl.when(i == 0)
    def _():
        ys_ref[...] = jnp.zeros_like(ys_ref)
        prime_copy(0).start()
        prime_copy(1).start()
        for r in range(MOE_BLOCK):
            gather_copy(tok_ref[0, r], r, 0).start()

    for r in range(MOE_BLOCK):
        gather_copy(tok_next_ref[0, r], r, 1 - cur).start(priority=r % 2)

    wait_gathers(cur)
    for j in range(ROW_TILES):
        xb_ref[:, j * LANES:(j + 1) * LANES] = xg_ref[cur, pl.ds(j, MOE_BLOCK, stride=ROW_TILES), :].astype(BF16)
    gu = _dot(xb_ref[...], wup_ref[...])
    hid = jax.nn.silu(gu[:, :D_EXPERT]) * gu[:, D_EXPERT:]
    y = _dot(hid.astype(BF16), wdn_ref[...])

    wait_scatters(cur)
    for j in range(ROW_TILES):
        ys_ref[cur, pl.ds(j, MOE_BLOCK, stride=ROW_TILES), :] = y[:, j * LANES:(j + 1) * LANES]
    for r in range(MOE_BLOCK):
        scatter_copy(slot_ref[0, r], r, cur).start(priority=r % 2)

    @pl.when(i == n - 1)
    def _():
        wait_scatters(cur)
        wait_scatters(1 - cur)
        wait_gathers(1 - cur)


def _expert_mlp(hn2, buf_tok, buf_slot, blk_e, n_slots, w_up, w_down):
    n_blk = blk_e.shape[0]
    idx_block = (None, 1, MOE_BLOCK)
    grid_spec = pltpu.PrefetchScalarGridSpec(
        num_scalar_prefetch=1,
        grid=(n_blk,),
        in_specs=[pl.BlockSpec(idx_block, lambda i, be: (i, 0, 0), memory_space=pltpu.SMEM),
                  pl.BlockSpec(idx_block, lambda i, be: (jnp.minimum(i + 1, n_blk - 1), 0, 0),
                               memory_space=pltpu.SMEM),
                  pl.BlockSpec(idx_block, lambda i, be: (i, 0, 0), memory_space=pltpu.SMEM),
                  pl.BlockSpec(memory_space=pl.ANY),
                  pl.BlockSpec((None, D_MODEL, 2 * D_EXPERT), lambda i, be: (be[i], 0, 0)),
                  pl.BlockSpec((None, D_EXPERT, D_MODEL), lambda i, be: (be[i], 0, 0))],
        out_specs=pl.BlockSpec(memory_space=pl.ANY),
        scratch_shapes=[pltpu.VMEM((2, MOE_BLOCK * ROW_TILES, LANES), F32),
                        pltpu.VMEM((2, MOE_BLOCK * ROW_TILES, LANES), F32),
                        pltpu.VMEM((MOE_BLOCK, D_MODEL), BF16),
                        pltpu.SemaphoreType.DMA((2,)), pltpu.SemaphoreType.DMA((2,))],
    )
    tok3 = buf_tok.reshape(n_blk, 1, MOE_BLOCK)
    return pl.pallas_call(
        functools.partial(_expert_kernel, n_slots),
        grid_spec=grid_spec,
        out_shape=jax.ShapeDtypeStruct(((n_slots + 2 * MOE_BLOCK) * ROW_TILES, LANES), F32),
        compiler_params=pltpu.CompilerParams(dimension_semantics=("arbitrary",), vmem_limit_bytes=VMEM_LIMIT),
        name="expert_mlp",
    )(blk_e, tok3, tok3, buf_slot.reshape(n_blk, 1, MOE_BLOCK), hn2, w_up, w_down)


def _moe(hn2, ids, counts, w_up, w_down):
    n_tok = ids.shape[0]
    m = n_tok * TOP_K
    p_rows = (m + N_EXPERTS * (MOE_BLOCK - 1) + MOE_BLOCK - 1) // MOE_BLOCK * MOE_BLOCK
    n_blk = p_rows // MOE_BLOCK
    eid = ids[:, 0:TOP_K]
    rank = ids[:, TOP_K:2 * TOP_K]
    cnt = counts[0, ROUTER_E0:ROUTER_E0 + N_EXPERTS].astype(jnp.int32)
    padded = (cnt + MOE_BLOCK - 1) // MOE_BLOCK * MOE_BLOCK
    pend = jnp.cumsum(padded)
    pstart = pend - padded
    dest = (pstart[eid] + rank).reshape(-1)
    pair_of_row = jnp.full((p_rows,), -1, jnp.int32).at[dest].set(jnp.arange(m, dtype=jnp.int32))
    is_pad = pair_of_row < 0
    pad_rank = jnp.cumsum(is_pad.astype(jnp.int32)) - 1
    tok_of_row = pair_of_row // TOP_K
    buf_slot = jnp.where(is_pad, m + pad_rank, (pair_of_row % TOP_K) * n_tok + tok_of_row)
    buf_tok = jnp.where(is_pad, 0, tok_of_row)
    blk_start = jnp.arange(n_blk, dtype=jnp.int32) * MOE_BLOCK
    blk_e = jnp.minimum(jnp.sum(pend[None, :] <= blk_start[:, None], axis=1), N_EXPERTS - 1).astype(jnp.int32)
    return _expert_mlp(hn2, buf_tok, buf_slot, blk_e, p_rows, w_up, w_down)


def _final_kernel(ts, x_ref, yt0_ref, yt1_ref, pg_ref, pmod_ref, gain_ref, o_ref):
    x = _residual_in(x_ref, (yt0_ref, yt1_ref, pg_ref, pmod_ref), ts)
    ms = jnp.mean(x * x, axis=-1, keepdims=True)
    o_ref[...] = x * lax.rsqrt(ms + EPS) * gain_ref[...]


def _final(x, prev, gain):
    yt, pgates, pmod = prev
    ts = TS_EVEN
    n_tok = x.shape[0]
    bsz = pmod.shape[0]
    n_s = n_tok // bsz // ts
    tok_map = lambda b, s: (b * n_s + s, 0)
    return pl.pallas_call(
        functools.partial(_final_kernel, ts),
        grid=(bsz, n_s),
        in_specs=[pl.BlockSpec((ts, D_MODEL), tok_map),
                  pl.BlockSpec((ts * ROW_TILES, LANES), tok_map),
                  pl.BlockSpec((ts * ROW_TILES, LANES), lambda b, s: (n_tok // ts + b * n_s + s, 0)),
                  pl.BlockSpec((ts, LANES), tok_map),
                  pl.BlockSpec((None, SUBLANES, D_MODEL), lambda b, s: (b, 0, 0)),
                  pl.BlockSpec((1, D_MODEL), lambda b, s: (0, 0))],
        out_specs=pl.BlockSpec((ts, D_MODEL), tok_map),
        out_shape=jax.ShapeDtypeStruct((n_tok, D_MODEL), F32),
        compiler_params=pltpu.CompilerParams(dimension_semantics=("arbitrary", "arbitrary"),
                                             vmem_limit_bytes=VMEM_LIMIT),
        name="final_norm",
    )(x, yt, yt, pgates, pmod, gain.reshape(1, -1))


def kernel(x, c, mod_w, mod_b, norm_mix, norm_ffn, ab_w_in, pool_w, pool_scale, conv_w, conv_b, conv_ln_g,
           conv_ln_b, ab_w_out, dn_w_in, dn_conv_w, dn_a_log, dn_dt_bias, dn_onorm, dn_w_out, moe_w_grp,
           moe_b_grp, moe_w_exp, moe_b_exp, moe_w_up, moe_w_down, final_norm):
    bsz, seq, d = x.shape
    depth = mod_w.shape[0]
    assert d == D_MODEL and seq % TS_EVEN == 0 and seq % TS_ODD == 0
    mod = _modulation(c, mod_w, mod_b)
    xt = x.reshape(bsz * seq, d)
    prev = None
    for l in range(depth):
        i = l // 2
        w_r = _split_hi_lo(jnp.pad(jnp.concatenate([moe_w_grp[l], moe_w_exp[l]], axis=1),
                                   ((0, 0), (0, LANES - N_GROUPS - N_EXPERTS))))
        b_r = jnp.pad(jnp.concatenate([moe_b_grp[l], moe_b_exp[l]]), (0, LANES - N_GROUPS - N_EXPERTS))
        b_r = b_r.reshape(1, -1)
        if l % 2 == 0:
            outs = _even_layer(xt, prev, mod[l], norm_mix[l], norm_ffn[l], ab_w_in[i], pool_w[i], pool_scale[i],
                               conv_w[i], conv_b[i], conv_ln_g[i], conv_ln_b[i], ab_w_out[i], w_r, b_r)
        else:
            outs = _odd_layer(xt, prev, mod[l], norm_mix[l], norm_ffn[l], dn_w_in[i], dn_conv_w[i], dn_a_log[i],
                              dn_dt_bias[i], dn_onorm[i], dn_w_out[i], w_r, b_r)
        xt, hn2, ids, gates, counts = outs
        yt = _moe(hn2, ids, counts, moe_w_up[l].astype(BF16), moe_w_down[l].astype(BF16))
        prev = (yt, gates, mod[l])
    out = _final(xt, prev, final_norm)
    return out.reshape(bsz, seq, d)
```

```python
import functools

import jax
import jax.numpy as jnp
from jax import lax
from jax.experimental import pallas as pl
from jax.experimental.pallas import tpu as pltpu

F32 = jnp.float32
BF16 = jnp.bfloat16
HI = lax.Precision.HIGHEST

D_MODEL = 1024
EPS = 1e-6
LANES = 128
SUBLANES = 8
ROW_TILES = D_MODEL // LANES
VMEM_LIMIT = 56 * 1024 * 1024

POOL_WINDOWS = (2, 4, 8, 16)
POOL_GROUP = 128
POOL_WIDTH = 512
CONV_WIDTH = 512
CONV_K = 31
AB_IN = POOL_WIDTH + 2 * CONV_WIDTH
HALO = 32

DN_HEADS = 8
DN_DIM = 128
DN_WIDTH = DN_HEADS * DN_DIM
DN_CONV_K = 4
DN_CHUNK = 128
DN_HALO = 8
DN_GROUP = 8
DN_INVERT = 8

N_GROUPS = 4
EPG = 8
N_EXPERTS = N_GROUPS * EPG
TOP_K = 2
D_EXPERT = 256
MOE_BLOCK = 256
ROUTER_E0 = N_GROUPS

TS_EVEN = 512
TS_ODD = 256

NEG = -1e30


def _dot(a, b):
    return jnp.dot(a, b, preferred_element_type=F32)


def _dot_nt(a, b):
    return lax.dot_general(a, b, (((1,), (1,)), ((), ())), preferred_element_type=F32)


def _dot_tn(a, b):
    return lax.dot_general(a, b, (((0,), (0,)), ((), ())), preferred_element_type=F32)


def _split_hi_lo(w):
    hi = w.astype(BF16)
    return jnp.stack([hi, (w - hi.astype(F32)).astype(BF16)])


def _dot_split(x, w_ref):
    xh = x.astype(BF16)
    xl = (x - xh.astype(F32)).astype(BF16)
    return _dot(xh, w_ref[0]) + (_dot(xl, w_ref[0]) + _dot(xh, w_ref[1]))


def _rms_mod(x, gain, scale, shift):
    ms = jnp.mean(x * x, axis=-1, keepdims=True)
    return x * lax.rsqrt(ms + EPS) * gain * (1.0 + scale) + shift


def _load_rows(ref, n, first, stride):
    return jnp.concatenate([ref[pl.ds(first + j, n, stride=stride), :] for j in range(ROW_TILES)], axis=1)


def _store_rows(ref, val, n):
    for j in range(ROW_TILES):
        ref[pl.ds(j, n, stride=ROW_TILES), :] = val[:, j * LANES:(j + 1) * LANES]


def _mod_kernel(c_ref, w_ref, b_ref, o_ref):
    ca = jax.nn.silu(c_ref[...])
    o_ref[...] = jnp.dot(ca, w_ref[...], precision=HI, preferred_element_type=F32) + b_ref[...]


def _modulation(c, mod_w, mod_b):
    depth = mod_w.shape[0]
    bsz = c.shape[0]
    n_col = mod_w.shape[2] // D_MODEL
    out = pl.pallas_call(
        _mod_kernel,
        grid=(depth, n_col),
        in_specs=[
            pl.BlockSpec((bsz, D_MODEL), lambda l, j: (0, 0)),
            pl.BlockSpec((None, D_MODEL, D_MODEL), lambda l, j: (l, 0, j)),
            pl.BlockSpec((None, 1, D_MODEL), lambda l, j: (l, 0, j)),
        ],
        out_specs=pl.BlockSpec((None, bsz, D_MODEL), lambda l, j: (l, 0, j)),
        out_shape=jax.ShapeDtypeStruct((depth, bsz, n_col * D_MODEL), F32),
        compiler_params=pltpu.CompilerParams(dimension_semantics=("arbitrary", "arbitrary"),
                                             vmem_limit_bytes=VMEM_LIMIT),
        name="modulation",
    )(c, mod_w, mod_b.reshape(depth, 1, n_col * D_MODEL))
    mod = out.reshape(depth, bsz, n_col, D_MODEL)
    return jnp.pad(mod, ((0, 0), (0, 0), (0, SUBLANES - n_col), (0, 0)))


def _residual_in(x_ref, prev, ts):
    x = x_ref[...]
    if prev is None:
        return x
    yt0_ref, yt1_ref, pg_ref, pmod_ref = prev
    y0 = _load_rows(yt0_ref, ts, 0, ROW_TILES)
    y1 = _load_rows(yt1_ref, ts, 0, ROW_TILES)
    pg = pg_ref[...]
    return x + pmod_ref[5:6, :] * (y0 * pg[:, 0:1] + y1 * pg[:, 1:2])


def _ffn_prep(x_new, first_step, mod_ref, nffn_ref, wr_ref, br_ref, cnt_ref, hn2_ref, ids_ref, gates_ref,
              counts_ref, ts):
    hn2 = _rms_mod(x_new, nffn_ref[...], mod_ref[4:5, :], mod_ref[3:4, :])
    _store_rows(hn2_ref, hn2, ts)
    lg = _dot_split(hn2, wr_ref) + br_ref[...]
    lane = lax.broadcasted_iota(jnp.int32, (ts, LANES), 1)
    is_grp = lane < N_GROUPS
    gl = jnp.where(is_grp, lg, NEG)
    gmax = jnp.max(gl, axis=-1, keepdims=True)
    gsum = jnp.sum(jnp.where(is_grp, jnp.exp(gl - gmax), 0.0), axis=-1, keepdims=True)
    gp = 1.0 / gsum
    gi = jnp.min(jnp.where(gl == gmax, lane, LANES), axis=-1, keepdims=True)
    lo = ROUTER_E0 + gi * EPG
    in_grp = (lane >= lo) & (lane < lo + EPG)
    el = jnp.where(in_grp, lg, NEG)
    m1 = jnp.max(el, axis=-1, keepdims=True)
    i1 = jnp.min(jnp.where(el == m1, lane, LANES), axis=-1, keepdims=True)
    el2 = jnp.where(lane == i1, NEG, el)
    m2 = jnp.max(el2, axis=-1, keepdims=True)
    i2 = jnp.min(jnp.where(el2 == m2, lane, LANES), axis=-1, keepdims=True)
    e2 = jnp.exp(m2 - m1)
    w1 = gp / (1.0 + e2)
    w2 = gp * e2 / (1.0 + e2)

    @pl.when(first_step)
    def _():
        cnt_ref[...] = jnp.zeros_like(cnt_ref)

    onehot = ((lane == i1) | (lane == i2)).astype(F32)
    r = lax.broadcasted_iota(jnp.int32, (ts, ts), 0)
    c = lax.broadcasted_iota(jnp.int32, (ts, ts), 1)
    before = (r > c).astype(BF16)
    run = _dot(before, onehot.astype(BF16)) + cnt_ref[0:1, :]
    rank1 = jnp.sum(jnp.where(lane == i1, run, 0.0), axis=-1, keepdims=True)
    rank2 = jnp.sum(jnp.where(lane == i2, run, 0.0), axis=-1, keepdims=True)
    total = cnt_ref[0:1, :] + jnp.sum(onehot, axis=0, keepdims=True)
    cnt_ref[...] = jnp.broadcast_to(total, cnt_ref.shape)
    counts_ref[...] = jnp.broadcast_to(total, counts_ref.shape)

    ids = jnp.where(lane == 0, i1 - ROUTER_E0,
                    jnp.where(lane == 1, i2 - ROUTER_E0,
                              jnp.where(lane == 2, rank1.astype(jnp.int32),
                                        jnp.where(lane == 3, rank2.astype(jnp.int32), 0))))
    ids_ref[...] = ids
    gates_ref[...] = jnp.where(lane == 0, w1, jnp.where(lane == 1, w2, 0.0))


def _mixer_call(body, ts, x, prev, mod_l, params, scratch, name):
    n_tok = x.shape[0]
    bsz = mod_l.shape[0]
    seq = n_tok // bsz
    n_s = seq // ts
    tok_map = lambda b, s: (b * n_s + s, 0)
    in_specs = [pl.BlockSpec((ts, D_MODEL), tok_map)]
    args = [x]
    if prev is not None:
        yt, pgates, pmod = prev
        n_tiles = n_tok // ts
        in_specs += [pl.BlockSpec((ts * ROW_TILES, LANES), tok_map),
                     pl.BlockSpec((ts * ROW_TILES, LANES), lambda b, s: (n_tiles + b * n_s + s, 0)),
                     pl.BlockSpec((ts, LANES), tok_map),
                     pl.BlockSpec((None, SUBLANES, D_MODEL), lambda b, s: (b, 0, 0))]
        args += [yt, yt, pgates, pmod]
    in_specs.append(pl.BlockSpec((None, SUBLANES, D_MODEL), lambda b, s: (b, 0, 0)))
    args.append(mod_l)
    for arr in params:
        nd = arr.ndim
        in_specs.append(pl.BlockSpec(arr.shape, lambda b, s, _nd=nd: (0,) * _nd))
        args.append(arr)
    out_shape = [
        jax.ShapeDtypeStruct((n_tok, D_MODEL), F32),
        jax.ShapeDtypeStruct((n_tok * ROW_TILES, LANES), F32),
        jax.ShapeDtypeStruct((n_tok, LANES), jnp.int32),
        jax.ShapeDtypeStruct((n_tok, LANES), F32),
        jax.ShapeDtypeStruct((SUBLANES, LANES), F32),
    ]
    out_specs = [
        pl.BlockSpec((ts, D_MODEL), tok_map),
        pl.BlockSpec((ts * ROW_TILES, LANES), tok_map),
        pl.BlockSpec((ts, LANES), tok_map),
        pl.BlockSpec((ts, LANES), tok_map),
        pl.BlockSpec((SUBLANES, LANES), lambda b, s: (0, 0)),
    ]
    return pl.pallas_call(
        functools.partial(body, prev is not None, ts),
        grid=(bsz, n_s),
        in_specs=in_specs,
        out_specs=out_specs,
        out_shape=out_shape,
        scratch_shapes=scratch,
        compiler_params=pltpu.CompilerParams(dimension_semantics=("arbitrary", "arbitrary"),
                                             vmem_limit_bytes=VMEM_LIMIT),
        name=name,
    )(*args)


def _even_kernel(has_prev, ts, *refs):
    n_prev = 4 if has_prev else 0
    x_ref = refs[0]
    prev = refs[1:1 + n_prev] if has_prev else None
    (mod_ref, nmix_ref, nffn_ref, win_ref, poolw_ref, pscale_ref, convw_ref, convb_ref, lng_ref, lnb_ref,
     wout_ref, wr_ref, br_ref) = refs[1 + n_prev:14 + n_prev]
    xo_ref, hn2_ref, ids_ref, gates_ref, counts_ref = refs[14 + n_prev:19 + n_prev]
    extp_ref, extc_ref, shift_ref, convo_ref, mix_ref, cnt_ref = refs[19 + n_prev:]

    b = pl.program_id(0)
    s = pl.program_id(1)
    x_in = _residual_in(x_ref, prev, ts)
    hn = _rms_mod(x_in, nmix_ref[...], mod_ref[1:2, :], mod_ref[0:1, :])
    u = _dot(hn.astype(BF16), win_ref[...])

    @pl.when(s == 0)
    def _():
        extp_ref[0:HALO, :] = jnp.zeros((HALO, POOL_WIDTH), F32)
        extc_ref[0:HALO, :] = jnp.zeros((HALO, CONV_WIDTH), F32)

    @pl.when(s > 0)
    def _():
        extp_ref[0:HALO, :] = extp_ref[ts:ts + HALO, :]
        extc_ref[0:HALO, :] = extc_ref[ts:ts + HALO, :]

    extp_ref[HALO:HALO + ts, :] = u[:, :POOL_WIDTH]
    extc_ref[HALO:HALO + ts, :] = (u[:, POOL_WIDTH:POOL_WIDTH + CONV_WIDTH]
                                   * jax.nn.sigmoid(u[:, POOL_WIDTH + CONV_WIDTH:]))

    pos = (s * ts + 1 + lax.broadcasted_iota(jnp.int32, (ts, 1), 0)).astype(F32)
    for j, w in enumerate(POOL_WINDOWS):
        c0 = j * POOL_GROUP
        a = extp_ref[HALO:HALO + ts, c0:c0 + POOL_GROUP]
        win = a
        for d in range(1, w):
            win = win + extp_ref[HALO - d:HALO - d + ts, c0:c0 + POOL_GROUP]
        pooled = win / jnp.minimum(pos, float(w)) - a
        ya = _dot(pooled.astype(BF16), poolw_ref[j]) * pscale_ref[:, c0:c0 + POOL_GROUP]
        mix_ref[:, c0:c0 + POOL_GROUP] = ya.astype(BF16)

    for sh in range(1, SUBLANES):
        shift_ref[sh - 1, SUBLANES:HALO + ts, :] = extc_ref[SUBLANES - sh:HALO + ts - sh, :]
    rc = 32

    def conv_chunk(ci, carry):
        base = pl.multiple_of(ci * rc, rc)
        acc = jnp.broadcast_to(convb_ref[...], (rc, CONV_WIDTH))
        for k in range(CONV_K):
            delay = CONV_K - 1 - k
            a8, sh = delay // SUBLANES, delay % SUBLANES
            rows = pl.ds(base + (HALO - a8 * SUBLANES), rc)
            tap = extc_ref[rows, :] if sh == 0 else shift_ref[sh - 1, rows, :]
            acc = acc + convw_ref[k:k + 1, :] * tap
        convo_ref[pl.ds(base, rc), :] = acc
        return carry

    lax.fori_loop(0, ts // rc, conv_chunk, 0)
    cv = convo_ref[...]
    mu = jnp.mean(cv, axis=-1, keepdims=True)
    cen = cv - mu
    var = jnp.mean(cen * cen, axis=-1, keepdims=True)
    yb = cen * lax.rsqrt(var + EPS) * lng_ref[...] + lnb_ref[...]
    mix_ref[:, POOL_WIDTH:POOL_WIDTH + CONV_WIDTH] = jax.nn.silu(yb).astype(BF16)

    y = _dot(mix_ref[...], wout_ref[...])
    x_new = x_in + mod_ref[2:3, :] * y
    xo_ref[...] = x_new
    _ffn_prep(x_new, (b == 0) & (s == 0), mod_ref, nffn_ref, wr_ref, br_ref, cnt_ref, hn2_ref, ids_ref,
              gates_ref, counts_ref, ts)


def _even_layer(x, prev, mod_l, nmix, nffn, w_in, pool_w, pool_scale, conv_w, conv_b, ln_g, ln_b, w_out, w_r,
                b_r):
    ts = TS_EVEN
    params = [nmix.reshape(1, -1), nffn.reshape(1, -1), w_in.astype(BF16), pool_w.astype(BF16),
              pool_scale.reshape(1, -1), jnp.pad(conv_w, ((0, HALO - CONV_K), (0, 0))), conv_b.reshape(1, -1),
              ln_g.reshape(1, -1), ln_b.reshape(1, -1), w_out.astype(BF16), w_r, b_r]
    scratch = [pltpu.VMEM((HALO + ts, POOL_WIDTH), F32), pltpu.VMEM((HALO + ts, CONV_WIDTH), F32),
               pltpu.VMEM((SUBLANES - 1, HALO + ts, CONV_WIDTH), F32), pltpu.VMEM((ts, CONV_WIDTH), F32),
               pltpu.VMEM((ts, D_MODEL), BF16), pltpu.VMEM((SUBLANES, LANES), F32)]
    return _mixer_call(_even_kernel, ts, x, prev, mod_l, params, scratch, "even_mixer")


def _odd_kernel(has_prev, ts, *refs):
    n_prev = 4 if has_prev else 0
    x_ref = refs[0]
    prev = refs[1:1 + n_prev] if has_prev else None
    (mod_ref, nmix_ref, nffn_ref, win_ref, wba_ref, convw_ref, alog_ref, dtb_ref, onorm_ref, wout_ref, wr_ref,
     br_ref) = refs[1 + n_prev:13 + n_prev]
    xo_ref, hn2_ref, ids_ref, gates_ref, counts_ref = refs[13 + n_prev:18 + n_prev]
    (ext_ref, q_ref, k_ref, v_ref, z_ref, gcol_ref, bcol_ref, grow_ref, og_ref, state_ref, mix_ref,
     cnt_ref) = refs[18 + n_prev:]

    b = pl.program_id(0)
    s = pl.program_id(1)
    n_ch = ts // DN_CHUNK
    x_in = _residual_in(x_ref, prev, ts)
    hn = _rms_mod(x_in, nmix_ref[...], mod_ref[1:2, :], mod_ref[0:1, :])
    hb = hn.astype(BF16)

    @pl.when(s == 0)
    def _():
        ext_ref[0:DN_HALO, :] = jnp.zeros((DN_HALO, 3 * DN_WIDTH), F32)
        state_ref[...] = jnp.zeros_like(state_ref)

    @pl.when(s > 0)
    def _():
        ext_ref[0:DN_HALO, :] = ext_ref[ts:ts + DN_HALO, :]

    for g in range(3):
        ext_ref[DN_HALO:DN_HALO + ts, g * DN_WIDTH:(g + 1) * DN_WIDTH] = _dot(
            hb, win_ref[:, g * DN_WIDTH:(g + 1) * DN_WIDTH])
    z = _dot(hb, win_ref[:, 3 * DN_WIDTH:4 * DN_WIDTH])
    for h in range(DN_HEADS):
        z_ref[h] = z[:, h * DN_DIM:(h + 1) * DN_DIM]

    for g in range(3):
        for h in range(DN_HEADS):
            c0 = g * DN_WIDTH + h * DN_DIM
            acc = None
            for k in range(DN_CONV_K):
                off = DN_HALO - (DN_CONV_K - 1) + k
                term = convw_ref[k:k + 1, c0:c0 + DN_DIM] * ext_ref[off:off + ts, c0:c0 + DN_DIM]
                acc = term if acc is None else acc + term
            cv = jax.nn.silu(acc)
            if g == 0:
                q_ref[h] = cv * lax.rsqrt(jnp.sum(cv * cv, axis=-1, keepdims=True) + EPS) * (DN_DIM ** -0.5)
            elif g == 1:
                k_ref[h] = cv * lax.rsqrt(jnp.sum(cv * cv, axis=-1, keepdims=True) + EPS)
            else:
                v_ref[h] = cv

    ba = _dot_split(hn, wba_ref)
    beta = jax.nn.sigmoid(ba)
    gdec = -jnp.exp(alog_ref[...]) * jax.nn.softplus(ba + dtb_ref[...])
    r = lax.broadcasted_iota(jnp.int32, (ts, ts), 0)
    c = lax.broadcasted_iota(jnp.int32, (ts, ts), 1)
    same_chunk = (r // DN_CHUNK) == (c // DN_CHUNK)
    tri = ((r >= c) & same_chunk).astype(BF16)
    g_hi = gdec.astype(BF16)
    g_r1 = gdec - g_hi.astype(F32)
    g_mid = g_r1.astype(BF16)
    g_lo = (g_r1 - g_mid.astype(F32)).astype(BF16)
    gc = _dot(tri, g_hi) + (_dot(tri, g_mid) + _dot(tri, g_lo))
    gct = gc.T
    for h in range(DN_HEADS):
        gcol_ref[h] = jnp.broadcast_to(gc[:, DN_HEADS + h:DN_HEADS + h + 1], (ts, LANES))
        bcol_ref[h] = jnp.broadcast_to(beta[:, h:h + 1], (ts, LANES))
        grow_ref[h] = jnp.broadcast_to(gct[DN_HEADS + h:DN_HEADS + h + 1, :], (SUBLANES, ts))

    ri = lax.broadcasted_iota(jnp.int32, (DN_CHUNK, DN_CHUNK), 0)
    ci = lax.broadcasted_iota(jnp.int32, (DN_CHUNK, DN_CHUNK), 1)
    causal = ri >= ci
    strict = ri > ci
    eye = (ri == ci).astype(F32)

    ci_pair = lax.broadcasted_iota(jnp.int32, (DN_CHUNK, 2 * DN_CHUNK), 1) & (DN_CHUNK - 1)
    ri_pair = lax.broadcasted_iota(jnp.int32, (DN_CHUNK, 2 * DN_CHUNK), 0)
    eye_pair = (ri_pair == ci_pair).astype(F32)
    zero_blk = jnp.zeros((DN_CHUNK, DN_CHUNK), BF16)

    def block_diag(p):
        return jnp.concatenate([jnp.concatenate([p[:, :DN_CHUNK], zero_blk], axis=1),
                                jnp.concatenate([zero_blk, p[:, DN_CHUNK:]], axis=1)], axis=0)

    def chunk_prep(h, ch):
        r0 = ch * DN_CHUNK
        q = q_ref[h, r0:r0 + DN_CHUNK, :]
        k = k_ref[h, r0:r0 + DN_CHUNK, :]
        gcl = gcol_ref[h, r0:r0 + DN_CHUNK, :]
        grw = grow_ref[h, 0:1, r0:r0 + DN_CHUNK]
        bt = bcol_ref[h, r0:r0 + DN_CHUNK, :]
        decay = jnp.where(causal, jnp.exp(jnp.where(causal, gcl - grw, 0.0)), 0.0)
        kbf = k.astype(BF16)
        nl = jnp.where(strict, -(_dot_nt((k * bt).astype(BF16), kbf) * decay), 0.0)
        attn = jnp.where(causal, _dot_nt(q.astype(BF16), kbf) * decay, 0.0)
        return nl, attn.astype(BF16)

    def state_free(h, ch, t_c):
        r0 = ch * DN_CHUNK
        q = q_ref[h, r0:r0 + DN_CHUNK, :]
        k = k_ref[h, r0:r0 + DN_CHUNK, :]
        v = v_ref[h, r0:r0 + DN_CHUNK, :]
        gcl = gcol_ref[h, r0:r0 + DN_CHUNK, :]
        bt = bcol_ref[h, r0:r0 + DN_CHUNK, :]
        eg = jnp.exp(gcl)
        rhs = jnp.concatenate([v * bt, k * bt * eg], axis=1).astype(BF16)
        uw = _dot(t_c.astype(BF16), rhs)
        glast = gcl[DN_CHUNK - 1:DN_CHUNK, :]
        kd = (k * jnp.exp(glast - gcl)).astype(BF16)
        return uw[:, :DN_DIM], uw[:, DN_DIM:].astype(BF16), (q * eg).astype(BF16), kd, jnp.exp(glast)

    def gate_out(h, ch, o):
        r0 = ch * DN_CHUNK
        zz = z_ref[h, r0:r0 + DN_CHUNK, :]
        o = o * lax.rsqrt(jnp.mean(o * o, axis=-1, keepdims=True) + EPS) * onorm_ref[...] * jax.nn.silu(zz)
        og_ref[h, r0:r0 + DN_CHUNK, :] = o

    def group_body(hg, carry):
        heads = [hg * DN_GROUP + g for g in range(DN_GROUP)]
        sts = [state_ref[h] for h in heads]
        for ch0 in range(0, n_ch, 2):
            preps, tms = [], []
            for sub in range(0, DN_GROUP, DN_INVERT):
                sub_preps = [(chunk_prep(h, ch0), chunk_prep(h, ch0 + 1)) for h in heads[sub:sub + DN_INVERT]]
                nl2s = [jnp.concatenate([p[0][0], p[1][0]], axis=1) for p in sub_preps]
                sub_tms = [eye_pair + jnp.where((ri_pair >> 1) == (ci_pair >> 1), nl2, 0.0) for nl2 in nl2s]
                blk = 2
                while blk < DN_CHUNK:
                    sel = (((ri_pair // (2 * blk)) == (ci_pair // (2 * blk)))
                           & ((ri_pair & (2 * blk - 1)) >= blk) & ((ci_pair & (2 * blk - 1)) < blk))
                    tmbs = [tm.astype(BF16) for tm in sub_tms]
                    ys = [_dot(jnp.where(sel, nl2, 0.0).astype(BF16), block_diag(tmb))
                          for nl2, tmb in zip(nl2s, tmbs)]
                    sub_tms = [tm + _dot(tmb, block_diag(y.astype(BF16)))
                               for tm, tmb, y in zip(sub_tms, tmbs, ys)]
                    blk *= 2
                preps += sub_preps
                tms += sub_tms
            free = [[state_free(h, ch0 + idx, tm[:, idx * DN_CHUNK:(idx + 1) * DN_CHUNK])
                     for h, tm in zip(heads, tms)] for idx in range(2)]
            for idx in range(2):
                sbs = [st.astype(BF16) for st in sts]
                w_s = [_dot(f[1], sb) for f, sb in zip(free[idx], sbs)]
                q_s = [_dot(f[2], sb) for f, sb in zip(free[idx], sbs)]
                vnbs = [(f[0] - ws).astype(BF16) for f, ws in zip(free[idx], w_s)]
                a_v = [_dot(p[idx][1], vnb) for p, vnb in zip(preps, vnbs)]
                k_v = [_dot_tn(f[3], vnb) for f, vnb in zip(free[idx], vnbs)]
                sts = [st * f[4] + kv for st, f, kv in zip(sts, free[idx], k_v)]
                for h, qs, av in zip(heads, q_s, a_v):
                    gate_out(h, ch0 + idx, qs + av)
        for h, st in zip(heads, sts):
            state_ref[h] = st
        return carry

    if DN_HEADS == DN_GROUP:
        group_body(0, 0)
    else:
        lax.fori_loop(0, DN_HEADS // DN_GROUP, group_body, 0)

    for h in range(DN_HEADS):
        mix_ref[:, h * DN_DIM:(h + 1) * DN_DIM] = og_ref[h].astype(BF16)
    y = _dot(mix_ref[...], wout_ref[...])
    x_new = x_in + mod_ref[2:3, :] * y
    xo_ref[...] = x_new
    _ffn_prep(x_new, (b == 0) & (s == 0), mod_ref, nffn_ref, wr_ref, br_ref, cnt_ref, hn2_ref, ids_ref,
              gates_ref, counts_ref, ts)


def _odd_layer(x, prev, mod_l, nmix, nffn, w_in, conv_w, a_log, dt_bias, onorm, w_out, w_r, b_r):
    ts = TS_ODD
    w_main = w_in[:, :4 * DN_WIDTH].astype(BF16)
    w_ba = _split_hi_lo(jnp.pad(w_in[:, 4 * DN_WIDTH:], ((0, 0), (0, LANES - 2 * DN_HEADS))))
    lane_pad = (DN_HEADS, LANES - 2 * DN_HEADS)
    params = [nmix.reshape(1, -1), nffn.reshape(1, -1), w_main, w_ba,
              jnp.pad(conv_w, ((0, SUBLANES - DN_CONV_K), (0, 0))),
              jnp.pad(a_log, lane_pad).reshape(1, -1), jnp.pad(dt_bias, lane_pad).reshape(1, -1),
              onorm.reshape(1, -1), w_out.astype(BF16), w_r, b_r]
    head = (DN_HEADS, ts, DN_DIM)
    scratch = [pltpu.VMEM((DN_HALO + ts, 3 * DN_WIDTH), F32),
               pltpu.VMEM(head, F32), pltpu.VMEM(head, F32), pltpu.VMEM(head, F32), pltpu.VMEM(head, F32),
               pltpu.VMEM(head, F32), pltpu.VMEM(head, F32), pltpu.VMEM((DN_HEADS, SUBLANES, ts), F32),
               pltpu.VMEM(head, F32), pltpu.VMEM((DN_HEADS, DN_DIM, DN_DIM), F32),
               pltpu.VMEM((ts, D_MODEL), BF16), pltpu.VMEM((SUBLANES, LANES), F32)]
    return _mixer_call(_odd_kernel, ts, x, prev, mod_l, params, scratch, "deltanet_mixer")


def _expert_kernel(n_slots, blk_e_ref, tok_ref, tok_next_ref, slot_ref, hn2_ref, wup_ref, wdn_ref, yt_ref,
                   xg_ref, ys_ref, xb_ref, sem_in, sem_out):
    i = pl.program_id(0)
    n = pl.num_programs(0)
    cur = i % 2

    def gather_copy(tok, r, buf):
        src = hn2_ref.at[pl.ds(pl.multiple_of(tok * ROW_TILES, ROW_TILES), ROW_TILES), :]
        return pltpu.make_async_copy(src, xg_ref.at[buf, pl.ds(r * ROW_TILES, ROW_TILES), :], sem_in.at[buf])

    def scatter_copy(slot, r, buf):
        dst = yt_ref.at[pl.ds(pl.multiple_of(slot * ROW_TILES, ROW_TILES), ROW_TILES), :]
        return pltpu.make_async_copy(ys_ref.at[buf, pl.ds(r * ROW_TILES, ROW_TILES), :], dst, sem_out.at[buf])

    block_rows = pl.ds(0, MOE_BLOCK * ROW_TILES)

    def wait_gathers(buf):
        pltpu.make_async_copy(hn2_ref.at[block_rows, :], xg_ref.at[buf], sem_in.at[buf]).wait()

    def wait_scatters(buf):
        pltpu.make_async_copy(ys_ref.at[buf], yt_ref.at[block_rows, :], sem_out.at[buf]).wait()

    def prime_copy(buf):
        spare = pl.ds((n_slots + buf * MOE_BLOCK) * ROW_TILES, MOE_BLOCK * ROW_TILES)
        return pltpu.make_async_copy(ys_ref.at[buf], yt_ref.at[spare, :], sem_out.at[buf])

    @pl.when(i == 0)
    def _():
        ys_ref[...] = jnp.zeros_like(ys_ref)
        prime_copy(0).start()
        prime_copy(1).start()
        for r in range(MOE_BLOCK):
            gather_copy(tok_ref[0, r], r, 0).start()

    for r in range(MOE_BLOCK):
        gather_copy(tok_next_ref[0, r], r, 1 - cur).start(priority=r % 2)

    wait_gathers(cur)
    for j in range(ROW_TILES):
        xb_ref[:, j * LANES:(j + 1) * LANES] = xg_ref[cur, pl.ds(j, MOE_BLOCK, stride=ROW_TILES), :].astype(BF16)
    gu = _dot(xb_ref[...], wup_ref[...])
    hid = jax.nn.silu(gu[:, :D_EXPERT]) * gu[:, D_EXPERT:]
    y = _dot(hid.astype(BF16), wdn_ref[...])

    wait_scatters(cur)
    for j in range(ROW_TILES):
        ys_ref[cur, pl.ds(j, MOE_BLOCK, stride=ROW_TILES), :] = y[:, j * LANES:(j + 1) * LANES]
    for r in range(MOE_BLOCK):
        scatter_copy(slot_ref[0, r], r, cur).start(priority=r % 2)

    @pl.when(i == n - 1)
    def _():
        wait_scatters(cur)
        wait_scatters(1 - cur)
        wait_gathers(1 - cur)


def _expert_mlp(hn2, buf_tok, buf_slot, blk_e, n_slots, w_up, w_down):
    n_blk = blk_e.shape[0]
    idx_block = (None, 1, MOE_BLOCK)
    grid_spec = pltpu.PrefetchScalarGridSpec(
        num_scalar_prefetch=1,
        grid=(n_blk,),
        in_specs=[pl.BlockSpec(idx_block, lambda i, be: (i, 0, 0), memory_space=pltpu.SMEM),
                  pl.BlockSpec(idx_block, lambda i, be: (jnp.minimum(i + 1, n_blk - 1), 0, 0),
                               memory_space=pltpu.SMEM),
                  pl.BlockSpec(idx_block, lambda i, be: (i, 0, 0), memory_space=pltpu.SMEM),
                  pl.BlockSpec(memory_space=pl.ANY),
                  pl.BlockSpec((None, D_MODEL, 2 * D_EXPERT), lambda i, be: (be[i], 0, 0)),
                  pl.BlockSpec((None, D_EXPERT, D_MODEL), lambda i, be: (be[i], 0, 0))],
        out_specs=pl.BlockSpec(memory_space=pl.ANY),
        scratch_shapes=[pltpu.VMEM((2, MOE_BLOCK * ROW_TILES, LANES), F32),
                        pltpu.VMEM((2, MOE_BLOCK * ROW_TILES, LANES), F32),
                        pltpu.VMEM((MOE_BLOCK, D_MODEL), BF16),
                        pltpu.SemaphoreType.DMA((2,)), pltpu.SemaphoreType.DMA((2,))],
    )
    tok3 = buf_tok.reshape(n_blk, 1, MOE_BLOCK)
    return pl.pallas_call(
        functools.partial(_expert_kernel, n_slots),
        grid_spec=grid_spec,
        out_shape=jax.ShapeDtypeStruct(((n_slots + 2 * MOE_BLOCK) * ROW_TILES, LANES), F32),
        compiler_params=pltpu.CompilerParams(dimension_semantics=("arbitrary",), vmem_limit_bytes=VMEM_LIMIT),
        name="expert_mlp",
    )(blk_e, tok3, tok3, buf_slot.reshape(n_blk, 1, MOE_BLOCK), hn2, w_up, w_down)


def _moe(hn2, ids, counts, w_up, w_down):
    n_tok = ids.shape[0]
    m = n_tok * TOP_K
    p_rows = (m + N_EXPERTS * (MOE_BLOCK - 1) + MOE_BLOCK - 1) // MOE_BLOCK * MOE_BLOCK
    n_blk = p_rows // MOE_BLOCK
    eid = ids[:, 0:TOP_K]
    rank = ids[:, TOP_K:2 * TOP_K]
    cnt = counts[0, ROUTER_E0:ROUTER_E0 + N_EXPERTS].astype(jnp.int32)
    padded = (cnt + MOE_BLOCK - 1) // MOE_BLOCK * MOE_BLOCK
    pend = jnp.cumsum(padded)
    pstart = pend - padded
    dest = (pstart[eid] + rank).reshape(-1)
    pair_of_row = jnp.zeros((p_rows,), jnp.int32).at[dest].set(jnp.arange(m, dtype=jnp.int32))
    blk_start = jnp.arange(n_blk, dtype=jnp.int32) * MOE_BLOCK
    blk_e = jnp.minimum(jnp.sum(pend[None, :] <= blk_start[:, None], axis=1), N_EXPERTS - 1).astype(jnp.int32)
    pad_before = jnp.cumsum(padded - cnt) - (padded - cnt)
    rank_in_e = (blk_start - pstart[blk_e])[:, None] + jnp.arange(MOE_BLOCK, dtype=jnp.int32)[None, :]
    n_real = cnt[blk_e][:, None]
    is_pad = (rank_in_e >= n_real).reshape(-1)
    pad_slot = (m + pad_before[blk_e][:, None] + rank_in_e - n_real).reshape(-1)
    tok_of_row = pair_of_row // TOP_K
    buf_slot = jnp.where(is_pad, pad_slot, (pair_of_row % TOP_K) * n_tok + tok_of_row)
    buf_tok = jnp.where(is_pad, 0, tok_of_row)
    return _expert_mlp(hn2, buf_tok, buf_slot, blk_e, p_rows, w_up, w_down)


def _final_kernel(ts, x_ref, yt0_ref, yt1_ref, pg_ref, pmod_ref, gain_ref, o_ref):
    x = _residual_in(x_ref, (yt0_ref, yt1_ref, pg_ref, pmod_ref), ts)
    ms = jnp.mean(x * x, axis=-1, keepdims=True)
    o_ref[...] = x * lax.rsqrt(ms + EPS) * gain_ref[...]


def _final(x, prev, gain):
    yt, pgates, pmod = prev
    ts = TS_EVEN
    n_tok = x.shape[0]
    bsz = pmod.shape[0]
    n_s = n_tok // bsz // ts
    tok_map = lambda b, s: (b * n_s + s, 0)
    return pl.pallas_call(
        functools.partial(_final_kernel, ts),
        grid=(bsz, n_s),
        in_specs=[pl.BlockSpec((ts, D_MODEL), tok_map),
                  pl.BlockSpec((ts * ROW_TILES, LANES), tok_map),
                  pl.BlockSpec((ts * ROW_TILES, LANES), lambda b, s: (n_tok // ts + b * n_s + s, 0)),
                  pl.BlockSpec((ts, LANES), tok_map),
                  pl.BlockSpec((None, SUBLANES, D_MODEL), lambda b, s: (b, 0, 0)),
                  pl.BlockSpec((1, D_MODEL), lambda b, s: (0, 0))],
        out_specs=pl.BlockSpec((ts, D_MODEL), tok_map),
        out_shape=jax.ShapeDtypeStruct((n_tok, D_MODEL), F32),
        compiler_params=pltpu.CompilerParams(dimension_semantics=("arbitrary", "arbitrary"),
                                             vmem_limit_bytes=VMEM_LIMIT),
        name="final_norm",
    )(x, yt, yt, pgates, pmod, gain.reshape(1, -1))


def kernel(x, c, mod_w, mod_b, norm_mix, norm_ffn, ab_w_in, pool_w, pool_scale, conv_w, conv_b, conv_ln_g,
           conv_ln_b, ab_w_out, dn_w_in, dn_conv_w, dn_a_log, dn_dt_bias, dn_onorm, dn_w_out, moe_w_grp,
           moe_b_grp, moe_w_exp, moe_b_exp, moe_w_up, moe_w_down, final_norm):
    bsz, seq, d = x.shape
    depth = mod_w.shape[0]
    assert d == D_MODEL and seq % TS_EVEN == 0 and seq % TS_ODD == 0
    mod = _modulation(c, mod_w, mod_b)
    xt = x.reshape(bsz * seq, d)
    prev = None
    for l in range(depth):
        i = l // 2
        w_r = _split_hi_lo(jnp.pad(jnp.concatenate([moe_w_grp[l], moe_w_exp[l]], axis=1),
                                   ((0, 0), (0, LANES - N_GROUPS - N_EXPERTS))))
        b_r = jnp.pad(jnp.concatenate([moe_b_grp[l], moe_b_exp[l]]), (0, LANES - N_GROUPS - N_EXPERTS))
        b_r = b_r.reshape(1, -1)
        if l % 2 == 0:
            outs = _even_layer(xt, prev, mod[l], norm_mix[l], norm_ffn[l], ab_w_in[i], pool_w[i], pool_scale[i],
                               conv_w[i], conv_b[i], conv_ln_g[i], conv_ln_b[i], ab_w_out[i], w_r, b_r)
        else:
            outs = _odd_layer(xt, prev, mod[l], norm_mix[l], norm_ffn[l], dn_w_in[i], dn_conv_w[i], dn_a_log[i],
                              dn_dt_bias[i], dn_onorm[i], dn_w_out[i], w_r, b_r)
        xt, hn2, ids, gates, counts = outs
        yt = _moe(hn2, ids, counts, moe_w_up[l].astype(BF16), moe_w_down[l].astype(BF16))
        prev = (yt, gates, mod[l])
    out = _final(xt, prev, final_norm)
    return out.reshape(bsz, seq, d)
```
